```python
import jax, jax.numpy as jnp
from jax import lax
import numpy as np

D_MODEL = 1024
BATCH = 4
SEQ = 8192
DEPTH = 1

N_META = 16
EPS = 1e-6
MIX_WIDTH = D_MODEL
CONV_CH = MIX_WIDTH // 2
CONV_HEADS = 8
CONV_WIDTH = 3
POOL_CH = MIX_WIDTH - CONV_CH
POOL_WINDOWS = (2, 4, 8, 16)
N_POOL_GROUPS = len(POOL_WINDOWS)
POOL_GROUP_DIM = POOL_CH // N_POOL_GROUPS
IN_PROJ_COLS = 3 * CONV_CH + POOL_CH

PEER_HEADS = 8
N_KEYS = 128
N_EXPERTS = N_KEYS * N_KEYS
PEER_KEY_DIM = 256
PEER_KEY_HALF = PEER_KEY_DIM // 2
PEER_TOPK = 16
PEER_CHUNK = 128

kernel_name = "hybrid_conv_pool_peer_encoder"


def rms_norm(x, g):
    xf = x.astype(jnp.float32)
    y = xf * lax.rsqrt(jnp.mean(xf * xf, axis=-1, keepdims=True) + EPS)
    return (y * g.astype(jnp.float32)).astype(x.dtype)


def short_gated_conv(h, b_gate, c_gate, conv_w, conv_b):
    L = h.shape[1]
    z = c_gate * h
    zp = jnp.pad(z, ((0, 0), (1, 1), (0, 0)))
    conv = (zp[:, 0:L] * conv_w[0] + zp[:, 1:L + 1] * conv_w[1]
            + zp[:, 2:L + 2] * conv_w[2] + conv_b)
    return b_gate * conv


def multiscale_pool(z, pool_w, pool_scale):
    Bn, L, C = z.shape
    zf = z.astype(jnp.float32).reshape(Bn, L, N_POOL_GROUPS, POOL_GROUP_DIM)
    cs = jnp.concatenate([jnp.zeros_like(zf[:, :1]), jnp.cumsum(zf, axis=1)], axis=1)
    t = np.arange(L)
    avgs = []
    for g, w in enumerate(POOL_WINDOWS):
        hi = np.minimum(t + w // 2, L)
        lo = np.maximum(t - w // 2, 0)
        cs_g = cs[:, :, g]
        s = jnp.take(cs_g, hi, axis=1) - jnp.take(cs_g, lo, axis=1)
        cnt = (hi - lo).astype(np.float32)
        avgs.append(s / cnt[None, :, None])
    avg = jnp.stack(avgs, axis=2)
    mixed = (avg - zf).astype(z.dtype)
    y = jnp.einsum('blgc,gcd->blgd', mixed, pool_w)
    return y.reshape(Bn, L, C) * pool_scale


def peer_ffn(xn, w_q, peer_keys, peer_u, peer_v):
    Bn, L, D = xn.shape
    tok = xn.reshape(-1, D)
    T = tok.shape[0]
    pad = (-T) % PEER_CHUNK
    chunks = jnp.pad(tok, ((0, pad), (0, 0))).reshape(-1, PEER_CHUNK, D)

    def chunk_fn(xc):
        c = xc.shape[0]
        q = (xc @ w_q).reshape(c, PEER_HEADS, 2, PEER_KEY_HALF)
        s = jnp.einsum('chpd,hpnd->chpn', q, peer_keys).astype(jnp.float32)
        sv, si = lax.top_k(s, PEER_TOPK)
        cand = sv[:, :, 0, :, None] + sv[:, :, 1, None, :]
        cand_idx = si[:, :, 0, :, None] * N_KEYS + si[:, :, 1, None, :]
        cand = cand.reshape(c, PEER_HEADS, PEER_TOPK * PEER_TOPK)
        cand_idx = cand_idx.reshape(c, PEER_HEADS, PEER_TOPK * PEER_TOPK)
        top_s, pos = lax.top_k(cand, PEER_TOPK)
        idx = jnp.take_along_axis(cand_idx, pos, axis=-1)
        gate = jax.nn.softmax(top_s, axis=-1)
        u_sel = peer_u[idx]
        v_sel = peer_v[idx]
        act = jax.nn.gelu(jnp.einsum('chkd,cd->chk', u_sel, xc).astype(jnp.float32),
                          approximate=False)
        wts = (gate * act).astype(xc.dtype)
        return jnp.einsum('chk,chkd->cd', wts, v_sel)

    out = lax.map(chunk_fn, chunks)
    return out.reshape(-1, D)[:T].reshape(Bn, L, D)


def setup_inputs(seed: int = 0) -> dict:
    key = jax.random.key(seed)
    ks = jax.random.split(key, 15)
    f32 = jnp.float32
    d = D_MODEL
    return {
        "x": jax.random.normal(ks[0], (BATCH, SEQ, d), f32),
        "meta_tokens": jax.random.normal(ks[1], (N_META, d), f32),
        "norm1_g": 1.0 + 0.02 * jax.random.normal(ks[2], (d,), f32),
        "w_in": jax.random.normal(ks[3], (d, IN_PROJ_COLS), f32) * d ** -0.5,
        "conv_w": jax.random.normal(ks[4], (CONV_WIDTH, CONV_CH), f32) * CONV_WIDTH ** -0.5,
        "conv_b": 0.01 * jax.random.normal(ks[5], (CONV_CH,), f32),
        "pool_w": jax.random.normal(ks[6], (N_POOL_GROUPS, POOL_GROUP_DIM, POOL_GROUP_DIM), f32)
                  * POOL_GROUP_DIM ** -0.5,
        "pool_scale": 1.0 + 0.1 * jax.random.normal(ks[7], (POOL_CH,), f32),
        "w_out": jax.random.normal(ks[8], (MIX_WIDTH, d), f32) * MIX_WIDTH ** -0.5,
        "norm2_g": 1.0 + 0.02 * jax.random.normal(ks[9], (d,), f32),
        "peer_w_q": jax.random.normal(ks[10], (d, PEER_HEADS * PEER_KEY_DIM), f32) * d ** -0.5,
        "peer_keys": jax.random.normal(ks[11], (PEER_HEADS, 2, N_KEYS, PEER_KEY_HALF), f32)
                     * PEER_KEY_HALF ** -0.5,
        "peer_u": jax.random.normal(ks[12], (N_EXPERTS, d), f32) * d ** -0.5,
        "peer_v": jax.random.normal(ks[13], (N_EXPERTS, d), f32) * PEER_HEADS ** -0.5,
        "final_norm_g": 1.0 + 0.02 * jax.random.normal(ks[14], (d,), f32),
    }


def reference(x, meta_tokens, norm1_g, w_in, conv_w, conv_b, pool_w, pool_scale, w_out,
              norm2_g, peer_w_q, peer_keys, peer_u, peer_v, final_norm_g):
    Bn = x.shape[0]
    meta = jnp.broadcast_to(meta_tokens.astype(x.dtype)[None], (Bn, N_META, D_MODEL))
    h = jnp.concatenate([meta, x], axis=1)

    for _ in range(DEPTH):
        hn = rms_norm(h, norm1_g)
        p = hn @ w_in
        hA = p[..., 0:CONV_CH]
        b_gate = p[..., CONV_CH:2 * CONV_CH]
        c_gate = p[..., 2 * CONV_CH:3 * CONV_CH]
        hB = p[..., 3 * CONV_CH:]
        yA = short_gated_conv(hA, b_gate, c_gate, conv_w, conv_b)
        yB = multiscale_pool(hB, pool_w, pool_scale)
        h = h + jnp.concatenate([yA, yB], axis=-1) @ w_out
        h = h + peer_ffn(rms_norm(h, norm2_g), peer_w_q, peer_keys, peer_u, peer_v)

    out = rms_norm(h, final_norm_g)
    return out[:, N_META:]
```

```python
import functools
import math

import jax
import jax.numpy as jnp
from jax import lax
from jax.experimental import pallas as pl
from jax.experimental.pallas import tpu as pltpu

N_META = 16
EPS = 1e-6
CONV_CH = 512
POOL_WINDOWS = (2, 4, 8, 16)
POOL_GROUP_DIM = 128
PEER_HEADS = 8
N_KEYS = 128
PEER_KEY_HALF = 128
PEER_TOPK = 16

LANES = 128
SUBLANES = 8
VMEM_LIMIT_BYTES = 56 * 1024 * 1024

HALO = 16
MIX_TILE = 512
ROUTE_TILE = SUBLANES * LANES
GROUP_PITCH = N_KEYS + SUBLANES
EXPERT_TILE_TOKENS = 512
EXPERT_ROWS_I = 16

_SQRT_HALF = math.sqrt(0.5)


def _rms_norm_f32(v, g):
    return v * lax.rsqrt(jnp.mean(v * v, axis=-1, keepdims=True) + EPS) * g


def _mixer_kernel(seq_len, xprev_ref, x_ref, xnext_ref, meta_ref, g1_ref, win_ref, convw_ref,
                  convb_ref, poolw_ref, pscale_ref, wout_ref, h1_ref, hn_scr, p_scr, z_scr, y_scr):
    ts = x_ref.shape[1]
    j = pl.program_id(1)
    last = pl.num_programs(1) - 1
    g1 = g1_ref[...]

    x = x_ref[0]
    prev = jnp.where(j == 0, meta_ref[...], xprev_ref[0])
    nxt = jnp.where(j == last, 0.0, _rms_norm_f32(xnext_ref[0], g1))
    hn_scr[0:HALO, :] = _rms_norm_f32(prev, g1).astype(jnp.bfloat16)
    hn_scr[HALO:HALO + ts, :] = _rms_norm_f32(x, g1).astype(jnp.bfloat16)
    hn_scr[HALO + ts:HALO + ts + HALO, :] = nxt.astype(jnp.bfloat16)

    p_scr[...] = jnp.dot(hn_scr[...], win_ref[...], preferred_element_type=jnp.float32)

    z_scr[...] = p_scr[:, 2 * CONV_CH:3 * CONV_CH] * p_scr[:, 0:CONV_CH]
    conv = (z_scr[HALO - 1:HALO - 1 + ts, :] * convw_ref[0:1, :]
            + z_scr[HALO:HALO + ts, :] * convw_ref[1:2, :]
            + z_scr[HALO + 1:HALO + 1 + ts, :] * convw_ref[2:3, :]
            + convb_ref[...])
    ya = p_scr[HALO:HALO + ts, CONV_CH:2 * CONV_CH] * conv
    y_scr[:, 0:CONV_CH] = ya.astype(jnp.bfloat16)

    pos = j * ts + lax.broadcasted_iota(jnp.int32, (ts, 1), 0)
    hb0 = 3 * CONV_CH
    for g, w in enumerate(POOL_WINDOWS):
        c0 = hb0 + g * POOL_GROUP_DIM
        c1 = c0 + POOL_GROUP_DIM
        acc = p_scr[HALO - w // 2:HALO - w // 2 + ts, c0:c1]
        for k in range(-w // 2 + 1, w // 2):
            acc = acc + p_scr[HALO + k:HALO + k + ts, c0:c1]
        cnt = (w - jnp.maximum(pos + w // 2 - seq_len, 0)).astype(jnp.float32)
        mixed = acc / cnt - p_scr[HALO:HALO + ts, c0:c1]
        yb = jnp.dot(mixed.astype(jnp.bfloat16), poolw_ref[g], preferred_element_type=jnp.float32)
        yb = yb * pscale_ref[:, g * POOL_GROUP_DIM:(g + 1) * POOL_GROUP_DIM]
        y_scr[:, CONV_CH + g * POOL_GROUP_DIM:CONV_CH + (g + 1) * POOL_GROUP_DIM] = yb.astype(jnp.bfloat16)

    h1_ref[0] = x + jnp.dot(y_scr[...], wout_ref[...], preferred_element_type=jnp.float32)


def _mixer(x, meta, g1, win, convw, convb, poolw, pscale, wout):
    b, seq, d = x.shape
    ts = MIX_TILE
    assert seq % ts == 0 and ts % HALO == 0 and meta.shape[0] == HALO
    nt = seq // ts
    hb = ts // HALO
    nhb = seq // HALO
    rows = ts + 2 * HALO
    full = lambda shape: pl.BlockSpec(shape, lambda bi, j: (0,) * len(shape),
                                      pipeline_mode=pl.Buffered(1))
    return pl.pallas_call(
        functools.partial(_mixer_kernel, seq),
        grid=(b, nt),
        in_specs=[
            pl.BlockSpec((1, HALO, d), lambda bi, j: (bi, jnp.maximum(j * hb - 1, 0), 0)),
            pl.BlockSpec((1, ts, d), lambda bi, j: (bi, j, 0)),
            pl.BlockSpec((1, HALO, d), lambda bi, j: (bi, jnp.minimum((j + 1) * hb, nhb - 1), 0)),
            full(meta.shape), full(g1.shape), full(win.shape), full(convw.shape), full(convb.shape),
            full(poolw.shape), full(pscale.shape), full(wout.shape),
        ],
        out_specs=pl.BlockSpec((1, ts, d), lambda bi, j: (bi, j, 0)),
        out_shape=jax.ShapeDtypeStruct((b, seq, d), jnp.float32),
        scratch_shapes=[
            pltpu.VMEM((rows, d), jnp.bfloat16),
            pltpu.VMEM((rows, win.shape[1]), jnp.float32),
            pltpu.VMEM((rows, CONV_CH), jnp.float32),
            pltpu.VMEM((ts, d), jnp.bfloat16),
        ],
        compiler_params=pltpu.CompilerParams(
            dimension_semantics=("arbitrary", "arbitrary"), vmem_limit_bytes=VMEM_LIMIT_BYTES),
        name="mixer",
    )(x, x, x, meta, g1, win, convw, convb, poolw, pscale, wout)


def _cmpx(v, a, b):
    hi = jnp.maximum(v[a], v[b])
    lo = jnp.minimum(v[a], v[b])
    v[a], v[b] = hi, lo


def _oddeven_merge_sort_pairs(n):
    pairs = []
    p = 1
    while p < n:
        k = p
        while k >= 1:
            for jj in range(k % p, n - k, 2 * k):
                for i in range(min(k, n - jj - k)):
                    if (i + jj) // (2 * p) == (i + jj + k) // (2 * p):
                        pairs.append((i + jj, i + jj + k))
            k //= 2
        p *= 2
    return pairs


_SORT16 = _oddeven_merge_sort_pairs(PEER_TOPK)


def _sort_desc(v):
    v = list(v)
    for a, b in _SORT16:
        _cmpx(v, a, b)
    return v


def _bitonic_sort_desc(v):
    v = list(v)
    d = len(v) // 2
    while d >= 1:
        for i in range(len(v)):
            if i & d == 0:
                _cmpx(v, i, i + d)
        d //= 2
    return v


def _top16_select(xs, ys):
    n = len(xs)
    out = list(xs)
    for m, y in enumerate(ys):
        out[n - 1 - m] = jnp.maximum(out[n - 1 - m], y)
    return out


def _routing_kernel(h1_ref, g2_ref, wqt_ref, keys_ref, xnt_ref, e0_ref, e1_ref, theta_ref,
                    xnb_scr, slab_scr):
    tr = h1_ref.shape[0]
    ngroups = tr // LANES
    xn = _rms_norm_f32(h1_ref[...], g2_ref[...])
    xnb_scr[...] = xn.astype(jnp.bfloat16)
    xnt_ref[...] = xn.T.astype(jnp.bfloat16)

    def slab_rows(g):
        return pl.ds(g * GROUP_PITCH, N_KEYS)

    def load_key(hp, n):
        return slab_scr[hp, pl.ds(n, ngroups, stride=GROUP_PITCH), :]

    def top16(hp):
        best = None
        for blk in range(N_KEYS // PEER_TOPK):
            cur = _sort_desc([load_key(hp, blk * PEER_TOPK + m) for m in range(PEER_TOPK)])
            best = cur if best is None else _bitonic_sort_desc(_top16_select(best, cur))
        return best

    def head_body(h, carry):
        q0 = pl.multiple_of(h * 2 * PEER_KEY_HALF, 2 * PEER_KEY_HALF)
        qt = lax.dot_general(wqt_ref[pl.ds(q0, 2 * PEER_KEY_HALF), :], xnb_scr[...],
                             (((1,), (1,)), ((), ())),
                             preferred_element_type=jnp.float32).astype(jnp.bfloat16)
        for half in range(2):
            s = jnp.dot(keys_ref[2 * h + half], qt[half * PEER_KEY_HALF:(half + 1) * PEER_KEY_HALF, :],
                        preferred_element_type=jnp.float32)
            for g in range(ngroups):
                slab_scr[half, slab_rows(g), :] = s[:, g * LANES:(g + 1) * LANES]
        a = top16(0)
        b = top16(1)
        cmax = a[0] + b[0]
        rows = [[a[i] + b[jj] for jj in range(PEER_TOPK // (i + 1))] for i in range(PEER_TOPK)]
        cands = [(i, jj, rows[i][jj]) for i in range(PEER_TOPK) for jj in range(len(rows[i]))]
        best = rows[0]
        singles = [rows[i][0] for i in range(PEER_TOPK // 2, PEER_TOPK)]
        merge_rows = [rows[i] for i in range(1, PEER_TOPK // 2)] + [singles]
        for r, row in enumerate(merge_rows):
            best = _top16_select(best, row)
            if r + 1 < len(merge_rows):
                best = _bitonic_sort_desc(best)
        tau = functools.reduce(jnp.minimum, best)
        z = functools.reduce(lambda u, v: u + v, [jnp.exp(c - cmax) for c in best])
        inv_z = 1.0 / z
        e0s = [jnp.exp(a[i] - a[0]) * inv_z for i in range(PEER_TOPK)]
        e1s = [jnp.exp(b[jj] - b[0]) for jj in range(PEER_TOPK)]
        theta = None
        for i, jj, c in cands:
            pij = jnp.where(c >= tau, e0s[i] * e1s[jj], jnp.inf)
            theta = pij if theta is None else jnp.minimum(theta, pij)
        for g in range(ngroups):
            lanes = slice(g * LANES, (g + 1) * LANES)
            a0 = a[0][g:g + 1, :]
            b0 = b[0][g:g + 1, :]
            iz = inv_z[g:g + 1, :]
            e0_ref[h, :, lanes] = jnp.exp(slab_scr[0, slab_rows(g), :] - a0) * iz
            e1_ref[h, :, lanes] = jnp.exp(slab_scr[1, slab_rows(g), :] - b0)
            theta_ref[h, :, lanes] = theta[g:g + 1, :]
        return carry

    lax.fori_loop(0, PEER_HEADS, head_body, 0)


def _routing(h1, g2, wqt, keys):
    t, d = h1.shape
    tr = ROUTE_TILE
    assert t % tr == 0
    ngroups = tr // LANES
    full = lambda shape: pl.BlockSpec(shape, lambda i: (0,) * len(shape),
                                      pipeline_mode=pl.Buffered(1))
    return pl.pallas_call(
        _routing_kernel,
        grid=(t // tr,),
        in_specs=[pl.BlockSpec((tr, d), lambda i: (i, 0)), full(g2.shape), full(wqt.shape),
                  full(keys.shape)],
        out_specs=[
            pl.BlockSpec((d, tr), lambda i: (0, i)),
            pl.BlockSpec((PEER_HEADS, N_KEYS, tr), lambda i: (0, 0, i)),
            pl.BlockSpec((PEER_HEADS, N_KEYS, tr), lambda i: (0, 0, i)),
            pl.BlockSpec((PEER_HEADS, 1, tr), lambda i: (0, 0, i)),
        ],
        out_shape=[
            jax.ShapeDtypeStruct((d, t), jnp.bfloat16),
            jax.ShapeDtypeStruct((PEER_HEADS, N_KEYS, t), jnp.float32),
            jax.ShapeDtypeStruct((PEER_HEADS, N_KEYS, t), jnp.float32),
            jax.ShapeDtypeStruct((PEER_HEADS, 1, t), jnp.float32),
        ],
        scratch_shapes=[
            pltpu.VMEM((tr, d), jnp.bfloat16),
            pltpu.VMEM((2, ngroups * GROUP_PITCH, LANES), jnp.float32),
        ],
        compiler_params=pltpu.CompilerParams(
            dimension_semantics=("arbitrary",), vmem_limit_bytes=VMEM_LIMIT_BYTES),
        name="routing",
    )(h1, g2, wqt, keys)


def _expert_kernel(xnt_ref, u_ref, vt_ref, e0_ref, e1_ref, theta_ref, h1_ref, gf_ref, out_ref,
                   a_scr, y_scr, acc_scr, e0b_scr, thb_scr):
    e = pl.program_id(1)
    tt = xnt_ref.shape[1]

    @pl.when(e == 0)
    def _():
        acc_scr[...] = jnp.zeros_like(acc_scr)

    a_scr[...] = jnp.dot(u_ref[...], xnt_ref[...], preferred_element_type=jnp.float32)

    for h in range(PEER_HEADS):
        for tg in range(tt // LANES):
            lanes = slice(tg * LANES, (tg + 1) * LANES)
            thb_scr[h, tg] = jnp.broadcast_to(theta_ref[h, :, lanes], (SUBLANES, LANES))
            for il in range(EXPERT_ROWS_I):
                e0b_scr[h, il, tg] = jnp.broadcast_to(e0_ref[h, il:il + 1, lanes], (SUBLANES, LANES))

    def row_body(il, carry):
        r0 = pl.multiple_of(il * N_KEYS, N_KEYS)
        for tg in range(tt // LANES):
            lanes = slice(tg * LANES, (tg + 1) * LANES)
            e0v = [e0b_scr[h, il, tg] for h in range(PEER_HEADS)]
            thv = [thb_scr[h, tg] for h in range(PEER_HEADS)]
            for jp in range(N_KEYS // (2 * SUBLANES)):
                ys = []
                for half in range(2):
                    j0 = (2 * jp + half) * SUBLANES
                    gate = None
                    for h in range(PEER_HEADS):
                        p = e0v[h] * e1_ref[h, j0:j0 + SUBLANES, lanes]
                        sel = jnp.where(p >= thv[h], p, 0.0)
                        gate = sel if gate is None else gate + sel
                    a = a_scr[pl.ds(r0 + j0, SUBLANES), lanes]
                    act = a * (0.5 + 0.5 * lax.erf(a * _SQRT_HALF))
                    ys.append(gate * act)
                y_scr[pl.ds(r0 + 2 * jp * SUBLANES, 2 * SUBLANES), lanes] = (
                    jnp.concatenate(ys, axis=0).astype(jnp.bfloat16))
        return carry

    lax.fori_loop(0, EXPERT_ROWS_I, row_body, 0)

    acc_scr[...] += jnp.dot(vt_ref[...], y_scr[...], preferred_element_type=jnp.float32)

    @pl.when(e == pl.num_programs(1) - 1)
    def _():
        h2 = h1_ref[...] + acc_scr[...].T
        out_ref[...] = _rms_norm_f32(h2, gf_ref[...])


def _experts(xnt, u, vt, e0, e1, theta, h1, gf):
    d, t = xnt.shape
    tt = EXPERT_TILE_TOKENS
    rows = EXPERT_ROWS_I * N_KEYS
    n_exp = u.shape[0]
    assert t % tt == 0 and n_exp % rows == 0
    return pl.pallas_call(
        _expert_kernel,
        grid=(t // tt, n_exp // rows),
        in_specs=[
            pl.BlockSpec((d, tt), lambda i, e: (0, i)),
            pl.BlockSpec((rows, d), lambda i, e: (e, 0)),
            pl.BlockSpec((d, rows), lambda i, e: (0, e)),
            pl.BlockSpec((PEER_HEADS, EXPERT_ROWS_I, tt), lambda i, e: (0, e, i)),
            pl.BlockSpec((PEER_HEADS, N_KEYS, tt), lambda i, e: (0, 0, i)),
            pl.BlockSpec((PEER_HEADS, 1, tt), lambda i, e: (0, 0, i)),
            pl.BlockSpec((tt, d), lambda i, e: (i, 0)),
            pl.BlockSpec(gf.shape, lambda i, e: (0, 0)),
        ],
        out_specs=pl.BlockSpec((tt, d), lambda i, e: (i, 0)),
        out_shape=jax.ShapeDtypeStruct((t, d), jnp.float32),
        scratch_shapes=[
            pltpu.VMEM((rows, tt), jnp.float32),
            pltpu.VMEM((rows, tt), jnp.bfloat16),
            pltpu.VMEM((d, tt), jnp.float32),
            pltpu.VMEM((PEER_HEADS, EXPERT_ROWS_I, tt // LANES, SUBLANES, LANES), jnp.float32),
            pltpu.VMEM((PEER_HEADS, tt // LANES, SUBLANES, LANES), jnp.float32),
        ],
        compiler_params=pltpu.CompilerParams(
            dimension_semantics=("arbitrary", "arbitrary"), vmem_limit_bytes=VMEM_LIMIT_BYTES),
        name="experts",
    )(xnt, u, vt, e0, e1, theta, h1, gf)


def kernel(x, meta_tokens, norm1_g, w_in, conv_w, conv_b, pool_w, pool_scale, w_out, norm2_g,
           peer_w_q, peer_keys, peer_u, peer_v, final_norm_g):
    b, seq, d = x.shape
    bf = jnp.bfloat16
    h1 = _mixer(x, meta_tokens, norm1_g.reshape(1, d), w_in.astype(bf), conv_w,
                conv_b.reshape(1, -1), pool_w.astype(bf), pool_scale.reshape(1, -1), w_out.astype(bf))
    h1 = h1.reshape(b * seq, d)
    keys = peer_keys.reshape(2 * PEER_HEADS, N_KEYS, PEER_KEY_HALF).astype(bf)
    xnt, e0, e1, theta = _routing(h1, norm2_g.reshape(1, d), peer_w_q.T.astype(bf), keys)
    out = _experts(xnt, peer_u.astype(bf), peer_v.T.astype(bf), e0, e1, theta, h1,
                   final_norm_g.reshape(1, d))
    return out.reshape(b, seq, d)
```

```python
import functools
import math

import jax
import jax.numpy as jnp
from jax import lax
from jax.experimental import pallas as pl
from jax.experimental.pallas import tpu as pltpu

N_META = 16
EPS = 1e-6
CONV_CH = 512
POOL_WINDOWS = (2, 4, 8, 16)
POOL_GROUP_DIM = 128
PEER_HEADS = 8
N_KEYS = 128
PEER_KEY_HALF = 128
PEER_TOPK = 16

LANES = 128
SUBLANES = 8
VMEM_LIMIT_BYTES = 56 * 1024 * 1024

HALO = 16
MIX_TILE = 512
ROUTE_TILE = SUBLANES * LANES
GROUP_PITCH = N_KEYS + SUBLANES
EXPERT_TILE_TOKENS = 512
EXPERT_ROWS_I = 16
EXPERT_CHUNK_I = 4

_SQRT_HALF = math.sqrt(0.5)


def _rms_norm_f32(v, g):
    return v * lax.rsqrt(jnp.mean(v * v, axis=-1, keepdims=True) + EPS) * g


def _mixer_kernel(seq_len, xprev_ref, x_ref, xnext_ref, meta_ref, g1_ref, win_ref, convw_ref,
                  convb_ref, poolw_ref, pscale_ref, wout_ref, h1_ref, hn_scr, p_scr, z_scr, y_scr):
    ts = x_ref.shape[1]
    j = pl.program_id(1)
    last = pl.num_programs(1) - 1
    g1 = g1_ref[...]

    x = x_ref[0]
    prev = jnp.where(j == 0, meta_ref[...], xprev_ref[0])
    nxt = jnp.where(j == last, 0.0, _rms_norm_f32(xnext_ref[0], g1))
    hn_scr[0:HALO, :] = _rms_norm_f32(prev, g1).astype(jnp.bfloat16)
    hn_scr[HALO:HALO + ts, :] = _rms_norm_f32(x, g1).astype(jnp.bfloat16)
    hn_scr[HALO + ts:HALO + ts + HALO, :] = nxt.astype(jnp.bfloat16)

    p_scr[...] = jnp.dot(hn_scr[...], win_ref[...], preferred_element_type=jnp.float32)

    z_scr[...] = p_scr[:, 2 * CONV_CH:3 * CONV_CH] * p_scr[:, 0:CONV_CH]
    conv = (z_scr[HALO - 1:HALO - 1 + ts, :] * convw_ref[0:1, :]
            + z_scr[HALO:HALO + ts, :] * convw_ref[1:2, :]
            + z_scr[HALO + 1:HALO + 1 + ts, :] * convw_ref[2:3, :]
            + convb_ref[...])
    ya = p_scr[HALO:HALO + ts, CONV_CH:2 * CONV_CH] * conv
    y_scr[:, 0:CONV_CH] = ya.astype(jnp.bfloat16)

    pos = j * ts + lax.broadcasted_iota(jnp.int32, (ts, 1), 0)
    hb0 = 3 * CONV_CH
    for g, w in enumerate(POOL_WINDOWS):
        c0 = hb0 + g * POOL_GROUP_DIM
        c1 = c0 + POOL_GROUP_DIM
        acc = p_scr[HALO - w // 2:HALO - w // 2 + ts, c0:c1]
        for k in range(-w // 2 + 1, w // 2):
            acc = acc + p_scr[HALO + k:HALO + k + ts, c0:c1]
        cnt = (w - jnp.maximum(pos + w // 2 - seq_len, 0)).astype(jnp.float32)
        mixed = acc / cnt - p_scr[HALO:HALO + ts, c0:c1]
        yb = jnp.dot(mixed.astype(jnp.bfloat16), poolw_ref[g], preferred_element_type=jnp.float32)
        yb = yb * pscale_ref[:, g * POOL_GROUP_DIM:(g + 1) * POOL_GROUP_DIM]
        y_scr[:, CONV_CH + g * POOL_GROUP_DIM:CONV_CH + (g + 1) * POOL_GROUP_DIM] = yb.astype(jnp.bfloat16)

    h1_ref[0] = x + jnp.dot(y_scr[...], wout_ref[...], preferred_element_type=jnp.float32)


def _mixer(x, meta, g1, win, convw, convb, poolw, pscale, wout):
    b, seq, d = x.shape
    ts = MIX_TILE
    assert seq % ts == 0 and ts % HALO == 0 and meta.shape[0] == HALO
    nt = seq // ts
    hb = ts // HALO
    nhb = seq // HALO
    rows = ts + 2 * HALO
    full = lambda shape: pl.BlockSpec(shape, lambda bi, j: (0,) * len(shape),
                                      pipeline_mode=pl.Buffered(1))
    return pl.pallas_call(
        functools.partial(_mixer_kernel, seq),
        grid=(b, nt),
        in_specs=[
            pl.BlockSpec((1, HALO, d), lambda bi, j: (bi, jnp.maximum(j * hb - 1, 0), 0)),
            pl.BlockSpec((1, ts, d), lambda bi, j: (bi, j, 0)),
            pl.BlockSpec((1, HALO, d), lambda bi, j: (bi, jnp.minimum((j + 1) * hb, nhb - 1), 0)),
            full(meta.shape), full(g1.shape), full(win.shape), full(convw.shape), full(convb.shape),
            full(poolw.shape), full(pscale.shape), full(wout.shape),
        ],
        out_specs=pl.BlockSpec((1, ts, d), lambda bi, j: (bi, j, 0)),
        out_shape=jax.ShapeDtypeStruct((b, seq, d), jnp.float32),
        scratch_shapes=[
            pltpu.VMEM((rows, d), jnp.bfloat16),
            pltpu.VMEM((rows, win.shape[1]), jnp.float32),
            pltpu.VMEM((rows, CONV_CH), jnp.float32),
            pltpu.VMEM((ts, d), jnp.bfloat16),
        ],
        compiler_params=pltpu.CompilerParams(
            dimension_semantics=("arbitrary", "arbitrary"), vmem_limit_bytes=VMEM_LIMIT_BYTES),
        name="mixer",
    )(x, x, x, meta, g1, win, convw, convb, poolw, pscale, wout)


def _cmpx(v, a, b):
    hi = jnp.maximum(v[a], v[b])
    lo = jnp.minimum(v[a], v[b])
    v[a], v[b] = hi, lo


def _oddeven_merge_sort_pairs(n):
    pairs = []
    p = 1
    while p < n:
        k = p
        while k >= 1:
            for jj in range(k % p, n - k, 2 * k):
                for i in range(min(k, n - jj - k)):
                    if (i + jj) // (2 * p) == (i + jj + k) // (2 * p):
                        pairs.append((i + jj, i + jj + k))
            k //= 2
        p *= 2
    return pairs


_SORT16 = _oddeven_merge_sort_pairs(PEER_TOPK)


def _sort_desc(v):
    v = list(v)
    for a, b in _SORT16:
        _cmpx(v, a, b)
    return v


def _bitonic_sort_desc(v):
    v = list(v)
    d = len(v) // 2
    while d >= 1:
        for i in range(len(v)):
            if i & d == 0:
                _cmpx(v, i, i + d)
        d //= 2
    return v


def _top16_select(xs, ys):
    n = len(xs)
    out = list(xs)
    for m, y in enumerate(ys):
        out[n - 1 - m] = jnp.maximum(out[n - 1 - m], y)
    return out


def _routing_kernel(h1_ref, g2_ref, wqt_ref, keys_ref, xnt_ref, e0_ref, e1_ref, theta_ref,
                    xnb_scr, slab_scr):
    tr = h1_ref.shape[0]
    ngroups = tr // LANES
    xn = _rms_norm_f32(h1_ref[...], g2_ref[...])
    xnb_scr[...] = xn.astype(jnp.bfloat16)
    xnt_ref[...] = xn.T.astype(jnp.bfloat16)

    def slab_rows(g):
        return pl.ds(g * GROUP_PITCH, N_KEYS)

    def load_key(hp, n):
        return slab_scr[hp, pl.ds(n, ngroups, stride=GROUP_PITCH), :]

    def top16(hp):
        best = None
        for blk in range(N_KEYS // PEER_TOPK):
            cur = _sort_desc([load_key(hp, blk * PEER_TOPK + m) for m in range(PEER_TOPK)])
            best = cur if best is None else _bitonic_sort_desc(_top16_select(best, cur))
        return best

    def head_body(h, carry):
        q0 = pl.multiple_of(h * 2 * PEER_KEY_HALF, 2 * PEER_KEY_HALF)
        qt = lax.dot_general(wqt_ref[pl.ds(q0, 2 * PEER_KEY_HALF), :], xnb_scr[...],
                             (((1,), (1,)), ((), ())),
                             preferred_element_type=jnp.float32).astype(jnp.bfloat16)
        for half in range(2):
            s = jnp.dot(keys_ref[2 * h + half], qt[half * PEER_KEY_HALF:(half + 1) * PEER_KEY_HALF, :],
                        preferred_element_type=jnp.float32)
            for g in range(ngroups):
                slab_scr[half, slab_rows(g), :] = s[:, g * LANES:(g + 1) * LANES]
        a = top16(0)
        b = top16(1)
        cmax = a[0] + b[0]
        rows = [[a[i] + b[jj] for jj in range(PEER_TOPK // (i + 1))] for i in range(PEER_TOPK)]
        cands = [(i, jj, rows[i][jj]) for i in range(PEER_TOPK) for jj in range(len(rows[i]))]
        best = rows[0]
        singles = [rows[i][0] for i in range(PEER_TOPK // 2, PEER_TOPK)]
        merge_rows = [rows[i] for i in range(1, PEER_TOPK // 2)] + [singles]
        for r, row in enumerate(merge_rows):
            best = _top16_select(best, row)
            if r + 1 < len(merge_rows):
                best = _bitonic_sort_desc(best)
        tau = functools.reduce(jnp.minimum, best)
        z = functools.reduce(lambda u, v: u + v, [jnp.exp(c - cmax) for c in best])
        inv_z = 1.0 / z
        e0s = [jnp.exp(a[i] - a[0]) * inv_z for i in range(PEER_TOPK)]
        e1s = [jnp.exp(b[jj] - b[0]) for jj in range(PEER_TOPK)]
        theta = None
        for i, jj, c in cands:
            pij = jnp.where(c >= tau, e0s[i] * e1s[jj], jnp.inf)
            theta = pij if theta is None else jnp.minimum(theta, pij)
        for g in range(ngroups):
            lanes = slice(g * LANES, (g + 1) * LANES)
            a0 = a[0][g:g + 1, :]
            b0 = b[0][g:g + 1, :]
            iz = inv_z[g:g + 1, :]
            e0_ref[h, :, lanes] = jnp.exp(slab_scr[0, slab_rows(g), :] - a0) * iz
            e1_ref[h, :, lanes] = jnp.exp(slab_scr[1, slab_rows(g), :] - b0)
            theta_ref[h, :, lanes] = theta[g:g + 1, :]
        return carry

    lax.fori_loop(0, PEER_HEADS, head_body, 0)


def _routing(h1, g2, wqt, keys):
    t, d = h1.shape
    tr = ROUTE_TILE
    assert t % tr == 0
    ngroups = tr // LANES
    full = lambda shape: pl.BlockSpec(shape, lambda i: (0,) * len(shape),
                                      pipeline_mode=pl.Buffered(1))
    return pl.pallas_call(
        _routing_kernel,
        grid=(t // tr,),
        in_specs=[pl.BlockSpec((tr, d), lambda i: (i, 0)), full(g2.shape), full(wqt.shape),
                  full(keys.shape)],
        out_specs=[
            pl.BlockSpec((d, tr), lambda i: (0, i)),
            pl.BlockSpec((PEER_HEADS, N_KEYS, tr), lambda i: (0, 0, i)),
            pl.BlockSpec((PEER_HEADS, N_KEYS, tr), lambda i: (0, 0, i)),
            pl.BlockSpec((PEER_HEADS, 1, tr), lambda i: (0, 0, i)),
        ],
        out_shape=[
            jax.ShapeDtypeStruct((d, t), jnp.bfloat16),
            jax.ShapeDtypeStruct((PEER_HEADS, N_KEYS, t), jnp.float32),
            jax.ShapeDtypeStruct((PEER_HEADS, N_KEYS, t), jnp.float32),
            jax.ShapeDtypeStruct((PEER_HEADS, 1, t), jnp.float32),
        ],
        scratch_shapes=[
            pltpu.VMEM((tr, d), jnp.bfloat16),
            pltpu.VMEM((2, ngroups * GROUP_PITCH, LANES), jnp.float32),
        ],
        compiler_params=pltpu.CompilerParams(
            dimension_semantics=("arbitrary",), vmem_limit_bytes=VMEM_LIMIT_BYTES),
        name="routing",
    )(h1, g2, wqt, keys)


def _expert_kernel(xnt_ref, u_ref, vt_ref, e0_ref, e1_ref, theta_ref, h1_ref, gf_ref, out_ref,
                   pre_scr, y_scr, acc_scr):
    e = pl.program_id(1)
    tt = xnt_ref.shape[1]

    @pl.when(e == 0)
    def _():
        acc_scr[...] = jnp.zeros_like(acc_scr)

    n_chunks = EXPERT_ROWS_I // EXPERT_CHUNK_I
    chunk_rows = EXPERT_CHUNK_I * N_KEYS

    def pre_activations(c):
        pre_scr[c % 2] = jnp.dot(u_ref[c * chunk_rows:(c + 1) * chunk_rows, :], xnt_ref[...],
                                 preferred_element_type=jnp.float32)

    def output_part(c):
        c0 = c * chunk_rows
        acc_scr[...] += jnp.dot(vt_ref[:, c0:c0 + chunk_rows], y_scr[c0:c0 + chunk_rows, :],
                                preferred_element_type=jnp.float32)

    pre_activations(0)
    for c in range(n_chunks + 1):
        for ic in range(EXPERT_CHUNK_I):
            if ic == 1 and c + 1 < n_chunks:
                pre_activations(c + 1)
            if ic == 2 and c >= 1:
                output_part(c - 1)
            if c == n_chunks:
                continue
            il = c * EXPERT_CHUNK_I + ic
            for tg in range(tt // LANES):
                lanes = slice(tg * LANES, (tg + 1) * LANES)
                e0v = [jnp.broadcast_to(e0_ref[h, il:il + 1, lanes], (SUBLANES, LANES))
                       for h in range(PEER_HEADS)]
                thv = [jnp.broadcast_to(theta_ref[h, :, lanes], (SUBLANES, LANES))
                       for h in range(PEER_HEADS)]
                for jp in range(N_KEYS // (2 * SUBLANES)):
                    ys = []
                    for half in range(2):
                        j0 = (2 * jp + half) * SUBLANES
                        gate = None
                        for h in range(PEER_HEADS):
                            p = e0v[h] * e1_ref[h, j0:j0 + SUBLANES, lanes]
                            sel = jnp.where(p >= thv[h], p, 0.0)
                            gate = sel if gate is None else gate + sel
                        a = pre_scr[c % 2, ic * N_KEYS + j0:ic * N_KEYS + j0 + SUBLANES, lanes]
                        act = a * (0.5 + 0.5 * lax.erf(a * _SQRT_HALF))
                        ys.append(gate * act)
                    r0 = il * N_KEYS + 2 * jp * SUBLANES
                    y_scr[r0:r0 + 2 * SUBLANES, lanes] = (
                        jnp.concatenate(ys, axis=0).astype(jnp.bfloat16))

    @pl.when(e == pl.num_programs(1) - 1)
    def _():
        h2 = h1_ref[...] + acc_scr[...].T
        out_ref[...] = _rms_norm_f32(h2, gf_ref[...])


def _experts(xnt, u, vt, e0, e1, theta, h1, gf):
    d, t = xnt.shape
    tt = EXPERT_TILE_TOKENS
    rows = EXPERT_ROWS_I * N_KEYS
    n_exp = u.shape[0]
    assert t % tt == 0 and n_exp % rows == 0
    return pl.pallas_call(
        _expert_kernel,
        grid=(t // tt, n_exp // rows),
        in_specs=[
            pl.BlockSpec((d, tt), lambda i, e: (0, i)),
            pl.BlockSpec((rows, d), lambda i, e: (e, 0)),
            pl.BlockSpec((d, rows), lambda i, e: (0, e)),
            pl.BlockSpec((PEER_HEADS, EXPERT_ROWS_I, tt), lambda i, e: (0, e, i)),
            pl.BlockSpec((PEER_HEADS, N_KEYS, tt), lambda i, e: (0, 0, i)),
            pl.BlockSpec((PEER_HEADS, 1, tt), lambda i, e: (0, 0, i)),
            pl.BlockSpec((tt, d), lambda i, e: (i, 0)),
            pl.BlockSpec(gf.shape, lambda i, e: (0, 0)),
        ],
        out_specs=pl.BlockSpec((tt, d), lambda i, e: (i, 0)),
        out_shape=jax.ShapeDtypeStruct((t, d), jnp.float32),
        scratch_shapes=[
            pltpu.VMEM((2, EXPERT_CHUNK_I * N_KEYS, tt), jnp.float32),
            pltpu.VMEM((rows, tt), jnp.bfloat16),
            pltpu.VMEM((d, tt), jnp.float32),
        ],
        compiler_params=pltpu.CompilerParams(
            dimension_semantics=("arbitrary", "arbitrary"), vmem_limit_bytes=VMEM_LIMIT_BYTES),
        name="experts",
    )(xnt, u, vt, e0, e1, theta, h1, gf)


def kernel(x, meta_tokens, norm1_g, w_in, conv_w, conv_b, pool_w, pool_scale, w_out, norm2_g,
           peer_w_q, peer_keys, peer_u, peer_v, final_norm_g):
    b, seq, d = x.shape
    bf = jnp.bfloat16
    h1 = _mixer(x, meta_tokens, norm1_g.reshape(1, d), w_in.astype(bf), conv_w,
                conv_b.reshape(1, -1), pool_w.astype(bf), pool_scale.reshape(1, -1), w_out.astype(bf))
    h1 = h1.reshape(b * seq, d)
    keys = peer_keys.reshape(2 * PEER_HEADS, N_KEYS, PEER_KEY_HALF).astype(bf)
    xnt, e0, e1, theta = _routing(h1, norm2_g.reshape(1, d), peer_w_q.T.astype(bf), keys)
    out = _experts(xnt, peer_u.astype(bf), peer_v.T.astype(bf), e0, e1, theta, h1,
                   final_norm_g.reshape(1, d))
    return out.reshape(b, seq, d)
```

```python
import functools
import math

import jax
import jax.numpy as jnp
from jax import lax
from jax.experimental import pallas as pl
from jax.experimental.pallas import tpu as pltpu

N_META = 16
EPS = 1e-6
CONV_CH = 512
POOL_WINDOWS = (2, 4, 8, 16)
POOL_GROUP_DIM = 128
PEER_HEADS = 8
N_KEYS = 128
PEER_KEY_HALF = 128
PEER_TOPK = 16

LANES = 128
SUBLANES = 8
MATMUL_LANES = 512
VMEM_LIMIT_BYTES = 56 * 1024 * 1024

HALO = 16
MIX_TILE = 512
PEER_TILE_TOKENS = SUBLANES * LANES
GROUP_PITCH = N_KEYS + SUBLANES
EXPERT_ROWS_I = 8
EXPERT_CHUNK_I = 4

_SQRT_HALF = math.sqrt(0.5)


def _rms_norm_f32(v, g):
    return v * lax.rsqrt(jnp.mean(v * v, axis=-1, keepdims=True) + EPS) * g


def _mixer_kernel(seq_len, xprev_ref, x_ref, xnext_ref, meta_ref, g1_ref, win_ref, convw_ref,
                  convb_ref, poolw_ref, pscale_ref, wout_ref, h1_ref, hn_scr, p_scr, z_scr, y_scr):
    ts = x_ref.shape[1]
    j = pl.program_id(1)
    last = pl.num_programs(1) - 1
    g1 = g1_ref[...]

    x = x_ref[0]
    prev = jnp.where(j == 0, meta_ref[...], xprev_ref[0])
    nxt = jnp.where(j == last, 0.0, _rms_norm_f32(xnext_ref[0], g1))
    hn_scr[0:HALO, :] = _rms_norm_f32(prev, g1).astype(jnp.bfloat16)
    hn_scr[HALO:HALO + ts, :] = _rms_norm_f32(x, g1).astype(jnp.bfloat16)
    hn_scr[HALO + ts:HALO + ts + HALO, :] = nxt.astype(jnp.bfloat16)

    p_scr[...] = jnp.dot(hn_scr[...], win_ref[...], preferred_element_type=jnp.float32)

    z_scr[...] = p_scr[:, 2 * CONV_CH:3 * CONV_CH] * p_scr[:, 0:CONV_CH]
    conv = (z_scr[HALO - 1:HALO - 1 + ts, :] * convw_ref[0:1, :]
            + z_scr[HALO:HALO + ts, :] * convw_ref[1:2, :]
            + z_scr[HALO + 1:HALO + 1 + ts, :] * convw_ref[2:3, :]
            + convb_ref[...])
    ya = p_scr[HALO:HALO + ts, CONV_CH:2 * CONV_CH] * conv
    y_scr[:, 0:CONV_CH] = ya.astype(jnp.bfloat16)

    pos = j * ts + lax.broadcasted_iota(jnp.int32, (ts, 1), 0)
    hb0 = 3 * CONV_CH
    for g, w in enumerate(POOL_WINDOWS):
        c0 = hb0 + g * POOL_GROUP_DIM
        c1 = c0 + POOL_GROUP_DIM
        acc = p_scr[HALO - w // 2:HALO - w // 2 + ts, c0:c1]
        for k in range(-w // 2 + 1, w // 2):
            acc = acc + p_scr[HALO + k:HALO + k + ts, c0:c1]
        cnt = (w - jnp.maximum(pos + w // 2 - seq_len, 0)).astype(jnp.float32)
        mixed = acc / cnt - p_scr[HALO:HALO + ts, c0:c1]
        yb = jnp.dot(mixed.astype(jnp.bfloat16), poolw_ref[g], preferred_element_type=jnp.float32)
        yb = yb * pscale_ref[:, g * POOL_GROUP_DIM:(g + 1) * POOL_GROUP_DIM]
        y_scr[:, CONV_CH + g * POOL_GROUP_DIM:CONV_CH + (g + 1) * POOL_GROUP_DIM] = yb.astype(jnp.bfloat16)

    h1_ref[0] = x + jnp.dot(y_scr[...], wout_ref[...], preferred_element_type=jnp.float32)


def _mixer(x, meta, g1, win, convw, convb, poolw, pscale, wout):
    b, seq, d = x.shape
    ts = MIX_TILE
    assert seq % ts == 0 and ts % HALO == 0 and meta.shape[0] == HALO
    nt = seq // ts
    hb = ts // HALO
    nhb = seq // HALO
    rows = ts + 2 * HALO
    full = lambda shape: pl.BlockSpec(shape, lambda bi, j: (0,) * len(shape),
                                      pipeline_mode=pl.Buffered(1))
    return pl.pallas_call(
        functools.partial(_mixer_kernel, seq),
        grid=(b, nt),
        in_specs=[
            pl.BlockSpec((1, HALO, d), lambda bi, j: (bi, jnp.maximum(j * hb - 1, 0), 0)),
            pl.BlockSpec((1, ts, d), lambda bi, j: (bi, j, 0)),
            pl.BlockSpec((1, HALO, d), lambda bi, j: (bi, jnp.minimum((j + 1) * hb, nhb - 1), 0)),
            full(meta.shape), full(g1.shape), full(win.shape), full(convw.shape), full(convb.shape),
            full(poolw.shape), full(pscale.shape), full(wout.shape),
        ],
        out_specs=pl.BlockSpec((1, ts, d), lambda bi, j: (bi, j, 0)),
        out_shape=jax.ShapeDtypeStruct((b, seq, d), jnp.float32),
        scratch_shapes=[
            pltpu.VMEM((rows, d), jnp.bfloat16),
            pltpu.VMEM((rows, win.shape[1]), jnp.float32),
            pltpu.VMEM((rows, CONV_CH), jnp.float32),
            pltpu.VMEM((ts, d), jnp.bfloat16),
        ],
        compiler_params=pltpu.CompilerParams(
            dimension_semantics=("arbitrary", "arbitrary"), vmem_limit_bytes=VMEM_LIMIT_BYTES),
        name="mixer",
    )(x, x, x, meta, g1, win, convw, convb, poolw, pscale, wout)


def _cmpx(v, a, b):
    hi = jnp.maximum(v[a], v[b])
    lo = jnp.minimum(v[a], v[b])
    v[a], v[b] = hi, lo


def _oddeven_merge_sort_pairs(n):
    pairs = []
    p = 1
    while p < n:
        k = p
        while k >= 1:
            for jj in range(k % p, n - k, 2 * k):
                for i in range(min(k, n - jj - k)):
                    if (i + jj) // (2 * p) == (i + jj + k) // (2 * p):
                        pairs.append((i + jj, i + jj + k))
            k //= 2
        p *= 2
    return pairs


_SORT16 = _oddeven_merge_sort_pairs(PEER_TOPK)


def _sort_desc(v):
    v = list(v)
    for a, b in _SORT16:
        _cmpx(v, a, b)
    return v


def _bitonic_sort_desc(v):
    v = list(v)
    d = len(v) // 2
    while d >= 1:
        for i in range(len(v)):
            if i & d == 0:
                _cmpx(v, i, i + d)
        d //= 2
    return v


def _top16_select(xs, ys):
    n = len(xs)
    out = list(xs)
    for m, y in enumerate(ys):
        out[n - 1 - m] = jnp.maximum(out[n - 1 - m], y)
    return out


def _route_tile(h, xnt_scr, wqt_ref, keys_ref, slab_scr, e0_scr, e1_scr, theta_scr):
    tt = xnt_scr.shape[1]
    ngroups = tt // LANES

    def slab_rows(g):
        return pl.ds(g * GROUP_PITCH, N_KEYS)

    def load_key(half, n):
        return slab_scr[half, pl.ds(n, ngroups, stride=GROUP_PITCH), :]

    def top16(half):
        best = None
        for blk in range(N_KEYS // PEER_TOPK):
            cur = _sort_desc([load_key(half, blk * PEER_TOPK + m) for m in range(PEER_TOPK)])
            best = cur if best is None else _bitonic_sort_desc(_top16_select(best, cur))
        return best

    q0 = pl.multiple_of(h * 2 * PEER_KEY_HALF, 2 * PEER_KEY_HALF)
    qt = jnp.dot(wqt_ref[pl.ds(q0, 2 * PEER_KEY_HALF), :], xnt_scr[...],
                 preferred_element_type=jnp.float32).astype(jnp.bfloat16)
    for half in range(2):
        s = jnp.dot(keys_ref[2 * h + half], qt[half * PEER_KEY_HALF:(half + 1) * PEER_KEY_HALF, :],
                    preferred_element_type=jnp.float32)
        for g in range(ngroups):
            slab_scr[half, slab_rows(g), :] = s[:, g * LANES:(g + 1) * LANES]
    a = top16(0)
    b = top16(1)
    cmax = a[0] + b[0]
    rows = [[a[i] + b[jj] for jj in range(PEER_TOPK // (i + 1))] for i in range(PEER_TOPK)]
    cands = [(i, jj, rows[i][jj]) for i in range(PEER_TOPK) for jj in range(len(rows[i]))]
    best = rows[0]
    singles = [rows[i][0] for i in range(PEER_TOPK // 2, PEER_TOPK)]
    merge_rows = [rows[i] for i in range(1, PEER_TOPK // 2)] + [singles]
    for r, row in enumerate(merge_rows):
        best = _top16_select(best, row)
        if r + 1 < len(merge_rows):
            best = _bitonic_sort_desc(best)
    tau = functools.reduce(jnp.minimum, best)
    z = functools.reduce(lambda u, v: u + v, [jnp.exp(c - cmax) for c in best])
    inv_z = 1.0 / z
    e0s = [jnp.exp(a[i] - a[0]) * inv_z for i in range(PEER_TOPK)]
    e1s = [jnp.exp(b[jj] - b[0]) for jj in range(PEER_TOPK)]
    theta = None
    for i, jj, c in cands:
        pij = jnp.where(c >= tau, e0s[i] * e1s[jj], jnp.inf)
        theta = pij if theta is None else jnp.minimum(theta, pij)
    for g in range(ngroups):
        lanes = slice(g * LANES, (g + 1) * LANES)
        a0 = a[0][g:g + 1, :]
        b0 = b[0][g:g + 1, :]
        iz = inv_z[g:g + 1, :]
        e0 = jnp.exp(slab_scr[0, slab_rows(g), :] - a0) * iz
        for blk in range(N_KEYS // EXPERT_ROWS_I):
            e0_scr[h, blk, :, lanes] = e0[blk * EXPERT_ROWS_I:(blk + 1) * EXPERT_ROWS_I, :]
        e1_scr[h, :, lanes] = jnp.exp(slab_scr[1, slab_rows(g), :] - b0)
        theta_scr[h, :, lanes] = theta[g:g + 1, :]


def _peer_kernel(h1_ref, g2_ref, wqt_ref, keys_ref, u_first_ref, u_next0_ref, u_next1_ref,
                 vt_prev0_ref, vt_prev1_ref, vt_last_ref, gf_ref, out_ref,
                 xnt_scr, slab_scr, e0_scr, e0cur_scr, e1_scr, theta_scr, pre0_scr, pre1_scr,
                 y0_scr, y1_scr, acc_scr):
    e = pl.program_id(1)
    tt = xnt_scr.shape[1]
    u_next = (u_next0_ref, u_next1_ref)
    vt_prev = (vt_prev0_ref, vt_prev1_ref)
    pre_scr = (pre0_scr, pre1_scr)
    y_scr = (y0_scr, y1_scr)
    assert EXPERT_ROWS_I == 2 * EXPERT_CHUNK_I and EXPERT_CHUNK_I == 2 * (tt // MATMUL_LANES)

    n_slabs = tt // MATMUL_LANES

    def pre_activations(u_chunk_ref, dst_scr, slab):
        n0 = slab * MATMUL_LANES
        dst_scr[:, n0:n0 + MATMUL_LANES] = jnp.dot(
            u_chunk_ref[...], xnt_scr[:, n0:n0 + MATMUL_LANES], preferred_element_type=jnp.float32)

    def output_product(vt_chunk_ref, y_chunk_scr, slab):
        n0 = slab * MATMUL_LANES
        acc_scr[:, n0:n0 + MATMUL_LANES] += jnp.dot(
            vt_chunk_ref[...], y_chunk_scr[:, n0:n0 + MATMUL_LANES],
            preferred_element_type=jnp.float32)

    @pl.when(e == 0)
    def _():
        for g in range(tt // LANES):
            rows = slice(g * LANES, (g + 1) * LANES)
            xn = _rms_norm_f32(h1_ref[rows, :], g2_ref[...])
            xnt_scr[:, rows] = xn.T.astype(jnp.bfloat16)

        def head_body(h, carry):
            _route_tile(h, xnt_scr, wqt_ref, keys_ref, slab_scr, e0_scr, e1_scr, theta_scr)
            return carry

        lax.fori_loop(0, PEER_HEADS, head_body, 0)
        acc_scr[...] = jnp.zeros_like(acc_scr)
        y_scr[1][...] = jnp.zeros_like(y_scr[1])
        for slab in range(n_slabs):
            pre_activations(u_first_ref, pre_scr[0], slab)

    e0cur_scr[...] = e0_scr[:, e]
    for j in range(2):
        for ic in range(EXPERT_CHUNK_I):
            if ic < n_slabs:
                pre_activations(u_next[j], pre_scr[(j + 1) % 2], ic)
            else:
                output_product(vt_prev[j], y_scr[(j + 1) % 2], ic - n_slabs)
            il = j * EXPERT_CHUNK_I + ic
            for tg in range(tt // LANES):
                lanes = slice(tg * LANES, (tg + 1) * LANES)
                e0v = [jnp.broadcast_to(e0cur_scr[h, il:il + 1, lanes], (SUBLANES, LANES))
                       for h in range(PEER_HEADS)]
                thv = [jnp.broadcast_to(theta_scr[h, :, lanes], (SUBLANES, LANES))
                       for h in range(PEER_HEADS)]
                for jp in range(N_KEYS // (2 * SUBLANES)):
                    ys = []
                    for half in range(2):
                        j0 = (2 * jp + half) * SUBLANES
                        gate = None
                        for h in range(PEER_HEADS):
                            p = e0v[h] * e1_scr[h, j0:j0 + SUBLANES, lanes]
                            sel = jnp.where(p >= thv[h], p, 0.0)
                            gate = sel if gate is None else gate + sel
                        a = pre_scr[j][ic * N_KEYS + j0:ic * N_KEYS + j0 + SUBLANES, lanes]
                        act = a * (0.5 + 0.5 * lax.erf(a * _SQRT_HALF))
                        ys.append(gate * act)
                    r0 = ic * N_KEYS + 2 * jp * SUBLANES
                    y_scr[j][r0:r0 + 2 * SUBLANES, lanes] = (
                        jnp.concatenate(ys, axis=0).astype(jnp.bfloat16))

    @pl.when(e == pl.num_programs(1) - 1)
    def _():
        for slab in range(n_slabs):
            output_product(vt_last_ref, y_scr[1], slab)
        for g in range(tt // LANES):
            rows = slice(g * LANES, (g + 1) * LANES)
            out_ref[rows, :] = _rms_norm_f32(h1_ref[rows, :] + acc_scr[:, rows].T, gf_ref[...])


def _peer(h1, g2, wqt, keys, u, vt, gf):
    t, d = h1.shape
    tt = PEER_TILE_TOKENS
    n_exp = u.shape[0]
    chunk_rows = EXPERT_CHUNK_I * N_KEYS
    n_chunks = n_exp // chunk_rows
    n_steps = N_KEYS // EXPERT_ROWS_I
    assert t % tt == 0 and tt == SUBLANES * LANES and n_chunks == 2 * n_steps
    ngroups = tt // LANES
    const = lambda shape: pl.BlockSpec(shape, lambda i, e: (0,) * len(shape),
                                       pipeline_mode=pl.Buffered(1))
    u_next = lambda j: pl.BlockSpec(
        (chunk_rows, d), lambda i, e: (jnp.minimum(2 * e + j + 1, n_chunks - 1), 0))
    vt_prev = lambda j: pl.BlockSpec(
        (d, chunk_rows), lambda i, e: (0, jnp.maximum(2 * e + j - 1, 0)))
    return pl.pallas_call(
        _peer_kernel,
        grid=(t // tt, n_steps),
        in_specs=[
            pl.BlockSpec((tt, d), lambda i, e: (i, 0)),
            const(g2.shape), const(wqt.shape), const(keys.shape),
            pl.BlockSpec((chunk_rows, d), lambda i, e: (0, 0), pipeline_mode=pl.Buffered(1)),
            u_next(0), u_next(1), vt_prev(0), vt_prev(1),
            pl.BlockSpec((d, chunk_rows), lambda i, e: (0, n_chunks - 1),
                         pipeline_mode=pl.Buffered(1)),
            const(gf.shape),
        ],
        out_specs=pl.BlockSpec((tt, d), lambda i, e: (i, 0), pipeline_mode=pl.Buffered(1)),
        out_shape=jax.ShapeDtypeStruct((t, d), jnp.float32),
        scratch_shapes=[
            pltpu.VMEM((d, tt), jnp.bfloat16),
            pltpu.VMEM((2, ngroups * GROUP_PITCH, LANES), jnp.float32),
            pltpu.VMEM((PEER_HEADS, n_steps, EXPERT_ROWS_I, tt), jnp.float32),
            pltpu.VMEM((PEER_HEADS, EXPERT_ROWS_I, tt), jnp.float32),
            pltpu.VMEM((PEER_HEADS, N_KEYS, tt), jnp.float32),
            pltpu.VMEM((PEER_HEADS, 1, tt), jnp.float32),
            pltpu.VMEM((chunk_rows, tt), jnp.float32),
            pltpu.VMEM((chunk_rows, tt), jnp.float32),
            pltpu.VMEM((chunk_rows, tt), jnp.bfloat16),
            pltpu.VMEM((chunk_rows, tt), jnp.bfloat16),
            pltpu.VMEM((d, tt), jnp.float32),
        ],
        compiler_params=pltpu.CompilerParams(
            dimension_semantics=("arbitrary", "arbitrary"), vmem_limit_bytes=VMEM_LIMIT_BYTES),
        name="peer",
    )(h1, g2, wqt, keys, u, u, u, vt, vt, vt, gf)


def kernel(x, meta_tokens, norm1_g, w_in, conv_w, conv_b, pool_w, pool_scale, w_out, norm2_g,
           peer_w_q, peer_keys, peer_u, peer_v, final_norm_g):
    b, seq, d = x.shape
    bf = jnp.bfloat16
    h1 = _mixer(x, meta_tokens, norm1_g.reshape(1, d), w_in.astype(bf), conv_w,
                conv_b.reshape(1, -1), pool_w.astype(bf), pool_scale.reshape(1, -1), w_out.astype(bf))
    keys = peer_keys.reshape(2 * PEER_HEADS, N_KEYS, PEER_KEY_HALF).astype(bf)
    out = _peer(h1.reshape(b * seq, d), norm2_g.reshape(1, d), peer_w_q.T.astype(bf), keys,
                peer_u.astype(bf), peer_v.T.astype(bf), final_norm_g.reshape(1, d))
    return out.reshape(b, seq, d)
```

```python
import functools
import math

import jax
import jax.numpy as jnp
from jax import lax
from jax.experimental import pallas as pl
from jax.experimental.pallas import tpu as pltpu

N_META = 16
EPS = 1e-6
CONV_CH = 512
POOL_WINDOWS = (2, 4, 8, 16)
POOL_GROUP_DIM = 128
PEER_HEADS = 8
N_KEYS = 128
PEER_KEY_HALF = 128
PEER_TOPK = 16

LANES = 128
SUBLANES = 8
MATMUL_LANES = 512
VMEM_LIMIT_BYTES = 56 * 1024 * 1024

HALO = 16
MIX_TILE = 512
PEER_TILE_TOKENS = SUBLANES * LANES
GROUP_PITCH = N_KEYS + SUBLANES
EXPERT_ROWS_I = 8
EXPERT_CHUNK_I = 4

_SQRT_HALF = math.sqrt(0.5)


def _rms_norm_f32(v, g):
    return v * lax.rsqrt(jnp.mean(v * v, axis=-1, keepdims=True) + EPS) * g


def _mixer_kernel(seq_len, xprev_ref, x_ref, xnext_ref, meta_ref, g1_ref, win_ref, convw_ref,
                  convb_ref, poolw_ref, pscale_ref, wout_ref, h1_ref, hn_scr, p_scr, z_scr, y_scr):
    ts = x_ref.shape[1]
    j = pl.program_id(1)
    last = pl.num_programs(1) - 1
    g1 = g1_ref[...]

    x = x_ref[0]
    prev = jnp.where(j == 0, meta_ref[...], xprev_ref[0])
    nxt = jnp.where(j == last, 0.0, _rms_norm_f32(xnext_ref[0], g1))
    hn_scr[0:HALO, :] = _rms_norm_f32(prev, g1).astype(jnp.bfloat16)
    hn_scr[HALO:HALO + ts, :] = _rms_norm_f32(x, g1).astype(jnp.bfloat16)
    hn_scr[HALO + ts:HALO + ts + HALO, :] = nxt.astype(jnp.bfloat16)

    p_scr[...] = jnp.dot(hn_scr[...], win_ref[...], preferred_element_type=jnp.float32)

    z_scr[...] = p_scr[:, 2 * CONV_CH:3 * CONV_CH] * p_scr[:, 0:CONV_CH]
    conv = (z_scr[HALO - 1:HALO - 1 + ts, :] * convw_ref[0:1, :]
            + z_scr[HALO:HALO + ts, :] * convw_ref[1:2, :]
            + z_scr[HALO + 1:HALO + 1 + ts, :] * convw_ref[2:3, :]
            + convb_ref[...])
    ya = p_scr[HALO:HALO + ts, CONV_CH:2 * CONV_CH] * conv
    y_scr[:, 0:CONV_CH] = ya.astype(jnp.bfloat16)

    pos = j * ts + lax.broadcasted_iota(jnp.int32, (ts, 1), 0)
    hb0 = 3 * CONV_CH
    for g, w in enumerate(POOL_WINDOWS):
        c0 = hb0 + g * POOL_GROUP_DIM
        c1 = c0 + POOL_GROUP_DIM
        acc = p_scr[HALO - w // 2:HALO - w // 2 + ts, c0:c1]
        for k in range(-w // 2 + 1, w // 2):
            acc = acc + p_scr[HALO + k:HALO + k + ts, c0:c1]
        cnt = (w - jnp.maximum(pos + w // 2 - seq_len, 0)).astype(jnp.float32)
        mixed = acc / cnt - p_scr[HALO:HALO + ts, c0:c1]
        yb = jnp.dot(mixed.astype(jnp.bfloat16), poolw_ref[g], preferred_element_type=jnp.float32)
        yb = yb * pscale_ref[:, g * POOL_GROUP_DIM:(g + 1) * POOL_GROUP_DIM]
        y_scr[:, CONV_CH + g * POOL_GROUP_DIM:CONV_CH + (g + 1) * POOL_GROUP_DIM] = yb.astype(jnp.bfloat16)

    h1_ref[0] = x + jnp.dot(y_scr[...], wout_ref[...], preferred_element_type=jnp.float32)


def _mixer(x, meta, g1, win, convw, convb, poolw, pscale, wout):
    b, seq, d = x.shape
    ts = MIX_TILE
    assert seq % ts == 0 and ts % HALO == 0 and meta.shape[0] == HALO
    nt = seq // ts
    hb = ts // HALO
    nhb = seq // HALO
    rows = ts + 2 * HALO
    full = lambda shape: pl.BlockSpec(shape, lambda bi, j: (0,) * len(shape),
                                      pipeline_mode=pl.Buffered(1))
    return pl.pallas_call(
        functools.partial(_mixer_kernel, seq),
        grid=(b, nt),
        in_specs=[
            pl.BlockSpec((1, HALO, d), lambda bi, j: (bi, jnp.maximum(j * hb - 1, 0), 0)),
            pl.BlockSpec((1, ts, d), lambda bi, j: (bi, j, 0)),
            pl.BlockSpec((1, HALO, d), lambda bi, j: (bi, jnp.minimum((j + 1) * hb, nhb - 1), 0)),
            full(meta.shape), full(g1.shape), full(win.shape), full(convw.shape), full(convb.shape),
            full(poolw.shape), full(pscale.shape), full(wout.shape),
        ],
        out_specs=pl.BlockSpec((1, ts, d), lambda bi, j: (bi, j, 0)),
        out_shape=jax.ShapeDtypeStruct((b, seq, d), jnp.float32),
        scratch_shapes=[
            pltpu.VMEM((rows, d), jnp.bfloat16),
            pltpu.VMEM((rows, win.shape[1]), jnp.float32),
            pltpu.VMEM((rows, CONV_CH), jnp.float32),
            pltpu.VMEM((ts, d), jnp.bfloat16),
        ],
        compiler_params=pltpu.CompilerParams(
            dimension_semantics=("arbitrary", "arbitrary"), vmem_limit_bytes=VMEM_LIMIT_BYTES),
        name="mixer",
    )(x, x, x, meta, g1, win, convw, convb, poolw, pscale, wout)


def _cmpx(v, a, b):
    hi = jnp.maximum(v[a], v[b])
    lo = jnp.minimum(v[a], v[b])
    v[a], v[b] = hi, lo


def _oddeven_merge_sort_pairs(n):
    pairs = []
    p = 1
    while p < n:
        k = p
        while k >= 1:
            for jj in range(k % p, n - k, 2 * k):
                for i in range(min(k, n - jj - k)):
                    if (i + jj) // (2 * p) == (i + jj + k) // (2 * p):
                        pairs.append((i + jj, i + jj + k))
            k //= 2
        p *= 2
    return pairs


_SORT16 = _oddeven_merge_sort_pairs(PEER_TOPK)


def _sort_desc(v):
    v = list(v)
    for a, b in _SORT16:
        _cmpx(v, a, b)
    return v


def _bitonic_sort_desc(v):
    v = list(v)
    d = len(v) // 2
    while d >= 1:
        for i in range(len(v)):
            if i & d == 0:
                _cmpx(v, i, i + d)
        d //= 2
    return v


def _top16_select(xs, ys):
    n = len(xs)
    out = list(xs)
    for m, y in enumerate(ys):
        out[n - 1 - m] = jnp.maximum(out[n - 1 - m], y)
    return out


def _route_tile(h, xnt_scr, wqt_ref, keys_ref, slab_scr, e0_scr, e1_scr, theta_scr):
    tt = xnt_scr.shape[1]
    ngroups = tt // LANES

    def slab_rows(g):
        return pl.ds(g * GROUP_PITCH, N_KEYS)

    def load_key(half, n):
        return slab_scr[half, pl.ds(n, ngroups, stride=GROUP_PITCH), :]

    def top16(half):
        best = None
        for blk in range(N_KEYS // PEER_TOPK):
            cur = _sort_desc([load_key(half, blk * PEER_TOPK + m) for m in range(PEER_TOPK)])
            best = cur if best is None else _bitonic_sort_desc(_top16_select(best, cur))
        return best

    q0 = pl.multiple_of(h * 2 * PEER_KEY_HALF, 2 * PEER_KEY_HALF)
    qt = jnp.dot(wqt_ref[pl.ds(q0, 2 * PEER_KEY_HALF), :], xnt_scr[...],
                 preferred_element_type=jnp.float32).astype(jnp.bfloat16)
    for half in range(2):
        s = jnp.dot(keys_ref[2 * h + half], qt[half * PEER_KEY_HALF:(half + 1) * PEER_KEY_HALF, :],
                    preferred_element_type=jnp.float32)
        for g in range(ngroups):
            slab_scr[half, slab_rows(g), :] = s[:, g * LANES:(g + 1) * LANES]
    a = top16(0)
    b = top16(1)
    cmax = a[0] + b[0]
    rows = [[a[i] + b[jj] for jj in range(PEER_TOPK // (i + 1))] for i in range(PEER_TOPK)]
    cands = [(i, jj, rows[i][jj]) for i in range(PEER_TOPK) for jj in range(len(rows[i]))]
    best = rows[0]
    singles = [rows[i][0] for i in range(PEER_TOPK // 2, PEER_TOPK)]
    merge_rows = [rows[i] for i in range(1, PEER_TOPK // 2)] + [singles]
    for r, row in enumerate(merge_rows):
        best = _top16_select(best, row)
        if r + 1 < len(merge_rows):
            best = _bitonic_sort_desc(best)
    tau = functools.reduce(jnp.minimum, best)
    z = functools.reduce(lambda u, v: u + v, [jnp.exp(c - cmax) for c in best])
    inv_z = 1.0 / z
    e0s = [jnp.exp(a[i] - a[0]) * inv_z for i in range(PEER_TOPK)]
    e1s = [jnp.exp(b[jj] - b[0]) for jj in range(PEER_TOPK)]
    theta = None
    for i, jj, c in cands:
        pij = jnp.where(c >= tau, e0s[i] * e1s[jj], jnp.inf)
        theta = pij if theta is None else jnp.minimum(theta, pij)
    for g in range(ngroups):
        lanes = slice(g * LANES, (g + 1) * LANES)
        a0 = a[0][g:g + 1, :]
        b0 = b[0][g:g + 1, :]
        iz = inv_z[g:g + 1, :]
        e0 = jnp.exp(slab_scr[0, slab_rows(g), :] - a0) * iz
        for blk in range(N_KEYS // EXPERT_ROWS_I):
            e0_scr[h, blk, :, lanes] = e0[blk * EXPERT_ROWS_I:(blk + 1) * EXPERT_ROWS_I, :]
        e1_scr[h, :, lanes] = jnp.exp(slab_scr[1, slab_rows(g), :] - b0)
        theta_scr[h, :, lanes] = theta[g:g + 1, :]


def _peer_kernel(h1_ref, g2_ref, wqt_ref, keys_ref, u_first_ref, u_next0_ref, u_next1_ref,
                 vt_prev0_ref, vt_prev1_ref, vt_last_ref, gf_ref, out_ref,
                 xnt_scr, slab_scr, e0_scr, e0cur_scr, e1_scr, theta_scr, pre0_scr, pre1_scr,
                 y0_scr, y1_scr, acc_scr):
    e = pl.program_id(1)
    tt = xnt_scr.shape[1]
    u_next = (u_next0_ref, u_next1_ref)
    vt_prev = (vt_prev0_ref, vt_prev1_ref)
    pre_scr = (pre0_scr, pre1_scr)
    y_scr = (y0_scr, y1_scr)
    assert EXPERT_ROWS_I == 2 * EXPERT_CHUNK_I and EXPERT_CHUNK_I == 2 * (tt // MATMUL_LANES)

    n_slabs = tt // MATMUL_LANES

    def pre_activations(u_chunk_ref, dst_scr, slab):
        n0 = slab * MATMUL_LANES
        dst_scr[:, n0:n0 + MATMUL_LANES] = jnp.dot(
            u_chunk_ref[...], xnt_scr[:, n0:n0 + MATMUL_LANES], preferred_element_type=jnp.float32)

    def output_product(vt_chunk_ref, y_chunk_scr, slab):
        n0 = slab * MATMUL_LANES
        acc_scr[:, n0:n0 + MATMUL_LANES] += jnp.dot(
            vt_chunk_ref[...], y_chunk_scr[:, n0:n0 + MATMUL_LANES],
            preferred_element_type=jnp.float32)

    @pl.when(e == 0)
    def _():
        for g in range(tt // LANES):
            rows = slice(g * LANES, (g + 1) * LANES)
            xn = _rms_norm_f32(h1_ref[rows, :], g2_ref[...])
            xnt_scr[:, rows] = xn.T.astype(jnp.bfloat16)

        def head_body(h, carry):
            _route_tile(h, xnt_scr, wqt_ref, keys_ref, slab_scr, e0_scr, e1_scr, theta_scr)
            return carry

        lax.fori_loop(0, PEER_HEADS, head_body, 0)
        acc_scr[...] = jnp.zeros_like(acc_scr)
        y_scr[1][...] = jnp.zeros_like(y_scr[1])
        for slab in range(n_slabs):
            pre_activations(u_first_ref, pre_scr[0], slab)

    e0cur_scr[...] = e0_scr[:, e]
    for j in range(2):
        for ic in range(EXPERT_CHUNK_I):
            if ic < n_slabs:
                pre_activations(u_next[j], pre_scr[(j + 1) % 2], ic)
            else:
                output_product(vt_prev[j], y_scr[(j + 1) % 2], ic - n_slabs)
            il = j * EXPERT_CHUNK_I + ic
            for tg in range(tt // LANES):
                lanes = slice(tg * LANES, (tg + 1) * LANES)
                e0v = [jnp.broadcast_to(e0cur_scr[h, il:il + 1, lanes], (SUBLANES, LANES))
                       for h in range(PEER_HEADS)]
                thv = [jnp.broadcast_to(theta_scr[h, :, lanes], (SUBLANES, LANES))
                       for h in range(PEER_HEADS)]
                for jp in range(N_KEYS // (2 * SUBLANES)):
                    ys = []
                    for half in range(2):
                        j0 = (2 * jp + half) * SUBLANES
                        gate = None
                        for h in range(PEER_HEADS):
                            p = e0v[h] * e1_scr[h, j0:j0 + SUBLANES, lanes]
                            sel = jnp.where(p >= thv[h], p, 0.0)
                            gate = sel if gate is None else gate + sel
                        a = pre_scr[j][ic * N_KEYS + j0:ic * N_KEYS + j0 + SUBLANES, lanes]
                        act = a * (0.5 + 0.5 * lax.erf(a * _SQRT_HALF))
                        ys.append(gate * act)
                    r0 = ic * N_KEYS + 2 * jp * SUBLANES
                    y_scr[j][r0:r0 + 2 * SUBLANES, lanes] = (
                        jnp.concatenate(ys, axis=0).astype(jnp.bfloat16))

    @pl.when(e == pl.num_programs(1) - 1)
    def _():
        for slab in range(n_slabs):
            output_product(vt_last_ref, y_scr[1], slab)
        for g in range(tt // LANES):
            rows = slice(g * LANES, (g + 1) * LANES)
            out_ref[rows, :] = _rms_norm_f32(h1_ref[rows, :] + acc_scr[:, rows].T, gf_ref[...])


def _peer(h1, g2, wqt, keys, u, vt, gf):
    t, d = h1.shape
    tt = PEER_TILE_TOKENS
    n_exp = u.shape[0]
    chunk_rows = EXPERT_CHUNK_I * N_KEYS
    n_chunks = n_exp // chunk_rows
    n_steps = N_KEYS // EXPERT_ROWS_I
    assert t % tt == 0 and tt == SUBLANES * LANES and n_chunks == 2 * n_steps
    ngroups = tt // LANES
    const = lambda shape: pl.BlockSpec(shape, lambda i, e: (0,) * len(shape),
                                       pipeline_mode=pl.Buffered(1))
    u_next = lambda j: pl.BlockSpec(
        (chunk_rows, d), lambda i, e: (jnp.minimum(2 * e + j + 1, n_chunks - 1), 0))
    vt_prev = lambda j: pl.BlockSpec(
        (None, d, chunk_rows), lambda i, e: (jnp.maximum(2 * e + j - 1, 0), 0, 0))
    return pl.pallas_call(
        _peer_kernel,
        grid=(t // tt, n_steps),
        in_specs=[
            pl.BlockSpec((tt, d), lambda i, e: (i, 0)),
            const(g2.shape), const(wqt.shape), const(keys.shape),
            pl.BlockSpec((chunk_rows, d), lambda i, e: (0, 0), pipeline_mode=pl.Buffered(1)),
            u_next(0), u_next(1), vt_prev(0), vt_prev(1),
            pl.BlockSpec((None, d, chunk_rows), lambda i, e: (n_chunks - 1, 0, 0),
                         pipeline_mode=pl.Buffered(1)),
            const(gf.shape),
        ],
        out_specs=pl.BlockSpec((tt, d), lambda i, e: (i, 0), pipeline_mode=pl.Buffered(1)),
        out_shape=jax.ShapeDtypeStruct((t, d), jnp.float32),
        scratch_shapes=[
            pltpu.VMEM((d, tt), jnp.bfloat16),
            pltpu.VMEM((2, ngroups * GROUP_PITCH, LANES), jnp.float32),
            pltpu.VMEM((PEER_HEADS, n_steps, EXPERT_ROWS_I, tt), jnp.float32),
            pltpu.VMEM((PEER_HEADS, EXPERT_ROWS_I, tt), jnp.float32),
            pltpu.VMEM((PEER_HEADS, N_KEYS, tt), jnp.float32),
            pltpu.VMEM((PEER_HEADS, 1, tt), jnp.float32),
            pltpu.VMEM((chunk_rows, tt), jnp.float32),
            pltpu.VMEM((chunk_rows, tt), jnp.float32),
            pltpu.VMEM((chunk_rows, tt), jnp.bfloat16),
            pltpu.VMEM((chunk_rows, tt), jnp.bfloat16),
            pltpu.VMEM((d, tt), jnp.float32),
        ],
        compiler_params=pltpu.CompilerParams(
            dimension_semantics=("arbitrary", "arbitrary"), vmem_limit_bytes=VMEM_LIMIT_BYTES),
        name="peer",
    )(h1, g2, wqt, keys, u, u, u, vt, vt, vt, gf)


def kernel(x, meta_tokens, norm1_g, w_in, conv_w, conv_b, pool_w, pool_scale, w_out, norm2_g,
           peer_w_q, peer_keys, peer_u, peer_v, final_norm_g):
    b, seq, d = x.shape
    bf = jnp.bfloat16
    h1 = _mixer(x, meta_tokens, norm1_g.reshape(1, d), w_in.astype(bf), conv_w,
                conv_b.reshape(1, -1), pool_w.astype(bf), pool_scale.reshape(1, -1), w_out.astype(bf))
    keys = peer_keys.reshape(2 * PEER_HEADS, N_KEYS, PEER_KEY_HALF).astype(bf)
    chunk_rows = EXPERT_CHUNK_I * N_KEYS
    vt = peer_v.astype(bf).reshape(-1, chunk_rows, d).transpose(0, 2, 1)
    out = _peer(h1.reshape(b * seq, d), norm2_g.reshape(1, d), peer_w_q.T.astype(bf), keys,
                peer_u.astype(bf), vt, final_norm_g.reshape(1, d))
    return out.reshape(b, seq, d)
```

```python
import functools
import math

import jax
import jax.numpy as jnp
from jax import lax
from jax.experimental import pallas as pl
from jax.experimental.pallas import tpu as pltpu

N_META = 16
EPS = 1e-6
CONV_CH = 512
POOL_WINDOWS = (2, 4, 8, 16)
POOL_GROUP_DIM = 128
PEER_HEADS = 8
N_KEYS = 128
PEER_KEY_HALF = 128
PEER_TOPK = 16

LANES = 128
SUBLANES = 8
VMEM_LIMIT_BYTES = 56 * 1024 * 1024

HALO = 16
MIX_TILE = 512
PEER_TILE_TOKENS = SUBLANES * LANES
GROUP_PITCH = N_KEYS + SUBLANES
EXPERT_ROWS_I = 8
EXPERT_CHUNK_I = 4

_SQRT_HALF = math.sqrt(0.5)


def _rms_norm_f32(v, g):
    return v * lax.rsqrt(jnp.mean(v * v, axis=-1, keepdims=True) + EPS) * g


def _mixer_kernel(seq_len, xprev_ref, x_ref, xnext_ref, meta_ref, g1_ref, win_ref, convw_ref,
                  convb_ref, poolw_ref, pscale_ref, wout_ref, h1_ref, hn_scr, p_scr, z_scr, y_scr):
    ts = x_ref.shape[1]
    j = pl.program_id(1)
    last = pl.num_programs(1) - 1
    g1 = g1_ref[...]

    x = x_ref[0]
    prev = jnp.where(j == 0, meta_ref[...], xprev_ref[0])
    nxt = jnp.where(j == last, 0.0, _rms_norm_f32(xnext_ref[0], g1))
    hn_scr[0:HALO, :] = _rms_norm_f32(prev, g1).astype(jnp.bfloat16)
    hn_scr[HALO:HALO + ts, :] = _rms_norm_f32(x, g1).astype(jnp.bfloat16)
    hn_scr[HALO + ts:HALO + ts + HALO, :] = nxt.astype(jnp.bfloat16)

    p_scr[...] = jnp.dot(hn_scr[...], win_ref[...], preferred_element_type=jnp.float32)

    z_scr[...] = p_scr[:, 2 * CONV_CH:3 * CONV_CH] * p_scr[:, 0:CONV_CH]
    conv = (z_scr[HALO - 1:HALO - 1 + ts, :] * convw_ref[0:1, :]
            + z_scr[HALO:HALO + ts, :] * convw_ref[1:2, :]
            + z_scr[HALO + 1:HALO + 1 + ts, :] * convw_ref[2:3, :]
            + convb_ref[...])
    ya = p_scr[HALO:HALO + ts, CONV_CH:2 * CONV_CH] * conv
    y_scr[:, 0:CONV_CH] = ya.astype(jnp.bfloat16)

    pos = j * ts + lax.broadcasted_iota(jnp.int32, (ts, 1), 0)
    hb0 = 3 * CONV_CH
    for g, w in enumerate(POOL_WINDOWS):
        c0 = hb0 + g * POOL_GROUP_DIM
        c1 = c0 + POOL_GROUP_DIM
        acc = p_scr[HALO - w // 2:HALO - w // 2 + ts, c0:c1]
        for k in range(-w // 2 + 1, w // 2):
            acc = acc + p_scr[HALO + k:HALO + k + ts, c0:c1]
        cnt = (w - jnp.maximum(pos + w // 2 - seq_len, 0)).astype(jnp.float32)
        mixed = acc / cnt - p_scr[HALO:HALO + ts, c0:c1]
        yb = jnp.dot(mixed.astype(jnp.bfloat16), poolw_ref[g], preferred_element_type=jnp.float32)
        yb = yb * pscale_ref[:, g * POOL_GROUP_DIM:(g + 1) * POOL_GROUP_DIM]
        y_scr[:, CONV_CH + g * POOL_GROUP_DIM:CONV_CH + (g + 1) * POOL_GROUP_DIM] = yb.astype(jnp.bfloat16)

    h1_ref[0] = x + jnp.dot(y_scr[...], wout_ref[...], preferred_element_type=jnp.float32)


def _mixer(x, meta, g1, win, convw, convb, poolw, pscale, wout):
    b, seq, d = x.shape
    ts = MIX_TILE
    assert seq % ts == 0 and ts % HALO == 0 and meta.shape[0] == HALO
    nt = seq // ts
    hb = ts // HALO
    nhb = seq // HALO
    rows = ts + 2 * HALO
    full = lambda shape: pl.BlockSpec(shape, lambda bi, j: (0,) * len(shape),
                                      pipeline_mode=pl.Buffered(1))
    return pl.pallas_call(
        functools.partial(_mixer_kernel, seq),
        grid=(b, nt),
        in_specs=[
            pl.BlockSpec((1, HALO, d), lambda bi, j: (bi, jnp.maximum(j * hb - 1, 0), 0)),
            pl.BlockSpec((1, ts, d), lambda bi, j: (bi, j, 0)),
            pl.BlockSpec((1, HALO, d), lambda bi, j: (bi, jnp.minimum((j + 1) * hb, nhb - 1), 0)),
            full(meta.shape), full(g1.shape), full(win.shape), full(convw.shape), full(convb.shape),
            full(poolw.shape), full(pscale.shape), full(wout.shape),
        ],
        out_specs=pl.BlockSpec((1, ts, d), lambda bi, j: (bi, j, 0)),
        out_shape=jax.ShapeDtypeStruct((b, seq, d), jnp.float32),
        scratch_shapes=[
            pltpu.VMEM((rows, d), jnp.bfloat16),
            pltpu.VMEM((rows, win.shape[1]), jnp.float32),
            pltpu.VMEM((rows, CONV_CH), jnp.float32),
            pltpu.VMEM((ts, d), jnp.bfloat16),
        ],
        compiler_params=pltpu.CompilerParams(
            dimension_semantics=("arbitrary", "arbitrary"), vmem_limit_bytes=VMEM_LIMIT_BYTES),
        name="mixer",
    )(x, x, x, meta, g1, win, convw, convb, poolw, pscale, wout)


def _cmpx(v, a, b):
    hi = jnp.maximum(v[a], v[b])
    lo = jnp.minimum(v[a], v[b])
    v[a], v[b] = hi, lo


def _oddeven_merge_sort_pairs(n):
    pairs = []
    p = 1
    while p < n:
        k = p
        while k >= 1:
            for jj in range(k % p, n - k, 2 * k):
                for i in range(min(k, n - jj - k)):
                    if (i + jj) // (2 * p) == (i + jj + k) // (2 * p):
                        pairs.append((i + jj, i + jj + k))
            k //= 2
        p *= 2
    return pairs


_SORT16 = _oddeven_merge_sort_pairs(PEER_TOPK)


def _sort_desc(v):
    v = list(v)
    for a, b in _SORT16:
        _cmpx(v, a, b)
    return v


def _bitonic_sort_desc(v):
    v = list(v)
    d = len(v) // 2
    while d >= 1:
        for i in range(len(v)):
            if i & d == 0:
                _cmpx(v, i, i + d)
        d //= 2
    return v


def _top16_select(xs, ys):
    n = len(xs)
    out = list(xs)
    for m, y in enumerate(ys):
        out[n - 1 - m] = jnp.maximum(out[n - 1 - m], y)
    return out


def _route_tile(h, xnt_scr, wqt_ref, keys_ref, slab_scr, e0_scr, e1_scr, theta_scr):
    tt = xnt_scr.shape[1]
    ngroups = tt // LANES

    def slab_rows(g):
        return pl.ds(g * GROUP_PITCH, N_KEYS)

    def load_key(half, n):
        return slab_scr[half, pl.ds(n, ngroups, stride=GROUP_PITCH), :]

    def top16(half):
        best = None
        for blk in range(N_KEYS // PEER_TOPK):
            cur = _sort_desc([load_key(half, blk * PEER_TOPK + m) for m in range(PEER_TOPK)])
            best = cur if best is None else _bitonic_sort_desc(_top16_select(best, cur))
        return best

    q0 = pl.multiple_of(h * 2 * PEER_KEY_HALF, 2 * PEER_KEY_HALF)
    qt = jnp.dot(wqt_ref[pl.ds(q0, 2 * PEER_KEY_HALF), :], xnt_scr[...],
                 preferred_element_type=jnp.float32).astype(jnp.bfloat16)
    for half in range(2):
        s = jnp.dot(keys_ref[2 * h + half], qt[half * PEER_KEY_HALF:(half + 1) * PEER_KEY_HALF, :],
                    preferred_element_type=jnp.float32)
        for g in range(ngroups):
            slab_scr[half, slab_rows(g), :] = s[:, g * LANES:(g + 1) * LANES]
    a = top16(0)
    b = top16(1)
    cmax = a[0] + b[0]
    rows = [[a[i] + b[jj] for jj in range(PEER_TOPK // (i + 1))] for i in range(PEER_TOPK)]
    cands = [(i, jj, rows[i][jj]) for i in range(PEER_TOPK) for jj in range(len(rows[i]))]
    best = rows[0]
    singles = [rows[i][0] for i in range(PEER_TOPK // 2, PEER_TOPK)]
    merge_rows = [rows[i] for i in range(1, PEER_TOPK // 2)] + [singles]
    for r, row in enumerate(merge_rows):
        best = _top16_select(best, row)
        if r + 1 < len(merge_rows):
            best = _bitonic_sort_desc(best)
    tau = functools.reduce(jnp.minimum, best)
    z = functools.reduce(lambda u, v: u + v, [jnp.exp(c - cmax) for c in best])
    inv_z = 0.5 / z
    e0s = [jnp.exp(a[i] - a[0]) * inv_z for i in range(PEER_TOPK)]
    e1s = [jnp.exp(b[jj] - b[0]) for jj in range(PEER_TOPK)]
    theta = None
    for i, jj, c in cands:
        pij = jnp.where(c >= tau, e0s[i] * e1s[jj], jnp.inf)
        theta = pij if theta is None else jnp.minimum(theta, pij)
    for g in range(ngroups):
        lanes = slice(g * LANES, (g + 1) * LANES)
        a0 = a[0][g:g + 1, :]
        b0 = b[0][g:g + 1, :]
        iz = inv_z[g:g + 1, :]
        e0 = jnp.exp(slab_scr[0, slab_rows(g), :] - a0) * iz
        for blk in range(N_KEYS // EXPERT_ROWS_I):
            e0_scr[h, blk, :, lanes] = e0[blk * EXPERT_ROWS_I:(blk + 1) * EXPERT_ROWS_I, :]
        e1_scr[h, :, lanes] = jnp.exp(slab_scr[1, slab_rows(g), :] - b0)
        theta_scr[h, :, lanes] = theta[g:g + 1, :]


def _peer_kernel(h1_ref, g2_ref, wqt_ref, keys_ref, u_first_ref, u_next0_ref, u_next1_ref,
                 vt_prev0_ref, vt_prev1_ref, vt_last_ref, gf_ref, out_ref,
                 xnt_scr, slab_scr, e0_scr, e1_scr, theta_scr, pre_scr, y_scr, acc_scr):
    e = pl.program_id(1)
    tt = xnt_scr.shape[1]
    u_next = (u_next0_ref, u_next1_ref)
    vt_prev = (vt_prev0_ref, vt_prev1_ref)
    assert EXPERT_ROWS_I == 2 * EXPERT_CHUNK_I

    @pl.when(e == 0)
    def _():
        for g in range(tt // LANES):
            rows = slice(g * LANES, (g + 1) * LANES)
            xn = _rms_norm_f32(h1_ref[rows, :], g2_ref[...])
            xnt_scr[:, rows] = xn.T.astype(jnp.bfloat16)

        def head_body(h, carry):
            _route_tile(h, xnt_scr, wqt_ref, keys_ref, slab_scr, e0_scr, e1_scr, theta_scr)
            return carry

        lax.fori_loop(0, PEER_HEADS, head_body, 0)
        acc_scr[...] = jnp.zeros_like(acc_scr)
        y_scr[1] = jnp.zeros(y_scr.shape[1:], y_scr.dtype)
        pre_scr[0] = jnp.dot(u_first_ref[...], xnt_scr[...], preferred_element_type=jnp.float32)

    for j in range(2):
        for ic in range(EXPERT_CHUNK_I):
            if ic == 1:
                pre_scr[(j + 1) % 2] = jnp.dot(u_next[j][...], xnt_scr[...],
                                               preferred_element_type=jnp.float32)
            if ic == 2:
                acc_scr[...] += jnp.dot(vt_prev[j][...], y_scr[(j + 1) % 2],
                                        preferred_element_type=jnp.float32)
            il = j * EXPERT_CHUNK_I + ic
            for tg in range(tt // LANES):
                lanes = slice(tg * LANES, (tg + 1) * LANES)
                e0v = [jnp.broadcast_to(e0_scr[h, e, il:il + 1, lanes], (SUBLANES, LANES))
                       for h in range(PEER_HEADS)]
                thv = [jnp.broadcast_to(theta_scr[h, :, lanes], (SUBLANES, LANES))
                       for h in range(PEER_HEADS)]
                for jp in range(N_KEYS // (2 * SUBLANES)):
                    ys = []
                    for half in range(2):
                        j0 = (2 * jp + half) * SUBLANES
                        gate = None
                        for h in range(PEER_HEADS):
                            p = e0v[h] * e1_scr[h, j0:j0 + SUBLANES, lanes]
                            sel = jnp.where(p >= thv[h], p, 0.0)
                            gate = sel if gate is None else gate + sel
                        a = pre_scr[j, ic * N_KEYS + j0:ic * N_KEYS + j0 + SUBLANES, lanes]
                        act = a * (1.0 + lax.erf(a * _SQRT_HALF))
                        ys.append(gate * act)
                    r0 = ic * N_KEYS + 2 * jp * SUBLANES
                    y_scr[j, r0:r0 + 2 * SUBLANES, lanes] = (
                        jnp.concatenate(ys, axis=0).astype(jnp.bfloat16))

    @pl.when(e == pl.num_programs(1) - 1)
    def _():
        acc_scr[...] += jnp.dot(vt_last_ref[...], y_scr[1], preferred_element_type=jnp.float32)
        for g in range(tt // LANES):
            rows = slice(g * LANES, (g + 1) * LANES)
            out_ref[rows, :] = _rms_norm_f32(h1_ref[rows, :] + acc_scr[:, rows].T, gf_ref[...])


def _peer(h1, g2, wqt, keys, u, vt, gf):
    t, d = h1.shape
    tt = PEER_TILE_TOKENS
    n_exp = u.shape[0]
    chunk_rows = EXPERT_CHUNK_I * N_KEYS
    n_chunks = n_exp // chunk_rows
    n_steps = N_KEYS // EXPERT_ROWS_I
    assert t % tt == 0 and tt == SUBLANES * LANES and n_chunks == 2 * n_steps
    ngroups = tt // LANES
    const = lambda shape: pl.BlockSpec(shape, lambda i, e: (0,) * len(shape),
                                       pipeline_mode=pl.Buffered(1))
    u_next = lambda j: pl.BlockSpec(
        (chunk_rows, d), lambda i, e: (jnp.minimum(2 * e + j + 1, n_chunks - 1), 0))
    vt_prev = lambda j: pl.BlockSpec(
        (None, d, chunk_rows), lambda i, e: (jnp.maximum(2 * e + j - 1, 0), 0, 0))
    return pl.pallas_call(
        _peer_kernel,
        grid=(t // tt, n_steps),
        in_specs=[
            pl.BlockSpec((tt, d), lambda i, e: (i, 0)),
            const(g2.shape), const(wqt.shape), const(keys.shape),
            pl.BlockSpec((chunk_rows, d), lambda i, e: (0, 0), pipeline_mode=pl.Buffered(1)),
            u_next(0), u_next(1), vt_prev(0), vt_prev(1),
            pl.BlockSpec((None, d, chunk_rows), lambda i, e: (n_chunks - 1, 0, 0),
                         pipeline_mode=pl.Buffered(1)),
            const(gf.shape),
        ],
        out_specs=pl.BlockSpec((tt, d), lambda i, e: (i, 0), pipeline_mode=pl.Buffered(1)),
        out_shape=jax.ShapeDtypeStruct((t, d), jnp.float32),
        scratch_shapes=[
            pltpu.VMEM((d, tt), jnp.bfloat16),
            pltpu.VMEM((2, ngroups * GROUP_PITCH, LANES), jnp.float32),
            pltpu.VMEM((PEER_HEADS, n_steps, EXPERT_ROWS_I, tt), jnp.float32),
            pltpu.VMEM((PEER_HEADS, N_KEYS, tt), jnp.float32),
            pltpu.VMEM((PEER_HEADS, 1, tt), jnp.float32),
            pltpu.VMEM((2, chunk_rows, tt), jnp.float32),
            pltpu.VMEM((2, chunk_rows, tt), jnp.bfloat16),
            pltpu.VMEM((d, tt), jnp.float32),
        ],
        compiler_params=pltpu.CompilerParams(
            dimension_semantics=("arbitrary", "arbitrary"), vmem_limit_bytes=VMEM_LIMIT_BYTES),
        name="peer",
    )(h1, g2, wqt, keys, u, u, u, vt, vt, vt, gf)


def kernel(x, meta_tokens, norm1_g, w_in, conv_w, conv_b, pool_w, pool_scale, w_out, norm2_g,
           peer_w_q, peer_keys, peer_u, peer_v, final_norm_g):
    b, seq, d = x.shape
    bf = jnp.bfloat16
    h1 = _mixer(x, meta_tokens, norm1_g.reshape(1, d), w_in.astype(bf), conv_w,
                conv_b.reshape(1, -1), pool_w.astype(bf), pool_scale.reshape(1, -1), w_out.astype(bf))
    keys = peer_keys.reshape(2 * PEER_HEADS, N_KEYS, PEER_KEY_HALF).astype(bf)
    chunk_rows = EXPERT_CHUNK_I * N_KEYS
    vt = peer_v.astype(bf).reshape(-1, chunk_rows, d).transpose(0, 2, 1)
    out = _peer(h1.reshape(b * seq, d), norm2_g.reshape(1, d), peer_w_q.T.astype(bf), keys,
                peer_u.astype(bf), vt, final_norm_g.reshape(1, d))
    return out.reshape(b, seq, d)
```

```python
import functools
import math

import jax
import jax.numpy as jnp
from jax import lax
from jax.experimental import pallas as pl
from jax.experimental.pallas import tpu as pltpu

N_META = 16
EPS = 1e-6
CONV_CH = 512
POOL_WINDOWS = (2, 4, 8, 16)
POOL_GROUP_DIM = 128
PEER_HEADS = 8
N_KEYS = 128
PEER_KEY_HALF = 128
PEER_TOPK = 16

LANES = 128
SUBLANES = 8
VMEM_LIMIT_BYTES = 56 * 1024 * 1024

HALO = 16
MIX_TILE = 1024
PEER_TILE_TOKENS = SUBLANES * LANES
GROUP_PITCH = N_KEYS + SUBLANES
EXPERT_ROWS_I = 8
EXPERT_CHUNK_I = 4

_SQRT_HALF = math.sqrt(0.5)


def _rms_norm_f32(v, g):
    return v * lax.rsqrt(jnp.mean(v * v, axis=-1, keepdims=True) + EPS) * g


def _mixer_kernel(seq_len, xprev_ref, x_ref, xnext_ref, meta_ref, g1_ref, win_ref, convw_ref,
                  convb_ref, poolw_ref, pscale_ref, wout_ref, h1_ref, hn_scr, p_scr, z_scr, y_scr):
    ts = x_ref.shape[1]
    j = pl.program_id(1)
    last = pl.num_programs(1) - 1
    g1 = g1_ref[...]

    x = x_ref[0]
    prev = jnp.where(j == 0, meta_ref[...], xprev_ref[0])
    nxt = jnp.where(j == last, 0.0, _rms_norm_f32(xnext_ref[0], g1))
    hn_scr[0:HALO, :] = _rms_norm_f32(prev, g1).astype(jnp.bfloat16)
    hn_scr[HALO:HALO + ts, :] = _rms_norm_f32(x, g1).astype(jnp.bfloat16)
    hn_scr[HALO + ts:HALO + ts + HALO, :] = nxt.astype(jnp.bfloat16)

    p_scr[...] = jnp.dot(hn_scr[...], win_ref[...], preferred_element_type=jnp.float32)

    z_scr[...] = p_scr[:, 2 * CONV_CH:3 * CONV_CH] * p_scr[:, 0:CONV_CH]
    conv = (z_scr[HALO - 1:HALO - 1 + ts, :] * convw_ref[0:1, :]
            + z_scr[HALO:HALO + ts, :] * convw_ref[1:2, :]
            + z_scr[HALO + 1:HALO + 1 + ts, :] * convw_ref[2:3, :]
            + convb_ref[...])
    ya = p_scr[HALO:HALO + ts, CONV_CH:2 * CONV_CH] * conv
    y_scr[:, 0:CONV_CH] = ya.astype(jnp.bfloat16)

    pos = j * ts + lax.broadcasted_iota(jnp.int32, (ts, 1), 0)
    hb0 = 3 * CONV_CH
    for g, w in enumerate(POOL_WINDOWS):
        c0 = hb0 + g * POOL_GROUP_DIM
        c1 = c0 + POOL_GROUP_DIM
        acc = p_scr[HALO - w // 2:HALO - w // 2 + ts, c0:c1]
        for k in range(-w // 2 + 1, w // 2):
            acc = acc + p_scr[HALO + k:HALO + k + ts, c0:c1]
        cnt = (w - jnp.maximum(pos + w // 2 - seq_len, 0)).astype(jnp.float32)
        mixed = acc / cnt - p_scr[HALO:HALO + ts, c0:c1]
        yb = jnp.dot(mixed.astype(jnp.bfloat16), poolw_ref[g], preferred_element_type=jnp.float32)
        yb = yb * pscale_ref[:, g * POOL_GROUP_DIM:(g + 1) * POOL_GROUP_DIM]
        y_scr[:, CONV_CH + g * POOL_GROUP_DIM:CONV_CH + (g + 1) * POOL_GROUP_DIM] = yb.astype(jnp.bfloat16)

    h1_ref[0] = x + jnp.dot(y_scr[...], wout_ref[...], preferred_element_type=jnp.float32)


def _mixer(x, meta, g1, win, convw, convb, poolw, pscale, wout):
    b, seq, d = x.shape
    ts = MIX_TILE
    assert seq % ts == 0 and ts % HALO == 0 and meta.shape[0] == HALO
    nt = seq // ts
    hb = ts // HALO
    nhb = seq // HALO
    rows = ts + 2 * HALO
    full = lambda shape: pl.BlockSpec(shape, lambda bi, j: (0,) * len(shape),
                                      pipeline_mode=pl.Buffered(1))
    return pl.pallas_call(
        functools.partial(_mixer_kernel, seq),
        grid=(b, nt),
        in_specs=[
            pl.BlockSpec((1, HALO, d), lambda bi, j: (bi, jnp.maximum(j * hb - 1, 0), 0)),
            pl.BlockSpec((1, ts, d), lambda bi, j: (bi, j, 0)),
            pl.BlockSpec((1, HALO, d), lambda bi, j: (bi, jnp.minimum((j + 1) * hb, nhb - 1), 0)),
            full(meta.shape), full(g1.shape), full(win.shape), full(convw.shape), full(convb.shape),
            full(poolw.shape), full(pscale.shape), full(wout.shape),
        ],
        out_specs=pl.BlockSpec((1, ts, d), lambda bi, j: (bi, j, 0)),
        out_shape=jax.ShapeDtypeStruct((b, seq, d), jnp.float32),
        scratch_shapes=[
            pltpu.VMEM((rows, d), jnp.bfloat16),
            pltpu.VMEM((rows, win.shape[1]), jnp.float32),
            pltpu.VMEM((rows, CONV_CH), jnp.float32),
            pltpu.VMEM((ts, d), jnp.bfloat16),
        ],
        compiler_params=pltpu.CompilerParams(
            dimension_semantics=("arbitrary", "arbitrary"), vmem_limit_bytes=VMEM_LIMIT_BYTES),
        name="mixer",
    )(x, x, x, meta, g1, win, convw, convb, poolw, pscale, wout)


def _cmpx(v, a, b):
    hi = jnp.maximum(v[a], v[b])
    lo = jnp.minimum(v[a], v[b])
    v[a], v[b] = hi, lo


def _oddeven_merge_sort_pairs(n):
    pairs = []
    p = 1
    while p < n:
        k = p
        while k >= 1:
            for jj in range(k % p, n - k, 2 * k):
                for i in range(min(k, n - jj - k)):
                    if (i + jj) // (2 * p) == (i + jj + k) // (2 * p):
                        pairs.append((i + jj, i + jj + k))
            k //= 2
        p *= 2
    return pairs


_SORT16 = _oddeven_merge_sort_pairs(PEER_TOPK)


def _sort_desc(v):
    v = list(v)
    for a, b in _SORT16:
        _cmpx(v, a, b)
    return v


def _bitonic_sort_desc(v):
    v = list(v)
    d = len(v) // 2
    while d >= 1:
        for i in range(len(v)):
            if i & d == 0:
                _cmpx(v, i, i + d)
        d //= 2
    return v


def _top16_select(xs, ys):
    n = len(xs)
    out = list(xs)
    for m, y in enumerate(ys):
        out[n - 1 - m] = jnp.maximum(out[n - 1 - m], y)
    return out


def _route_tile(h, xnt_scr, wqt_ref, keys_ref, slab_scr, e0_scr, e1_scr, theta_scr):
    tt = xnt_scr.shape[1]
    ngroups = tt // LANES

    def slab_rows(g):
        return pl.ds(g * GROUP_PITCH, N_KEYS)

    def load_key(half, n):
        return slab_scr[half, pl.ds(n, ngroups, stride=GROUP_PITCH), :]

    def top16(half):
        best = None
        for blk in range(N_KEYS // PEER_TOPK):
            cur = _sort_desc([load_key(half, blk * PEER_TOPK + m) for m in range(PEER_TOPK)])
            best = cur if best is None else _bitonic_sort_desc(_top16_select(best, cur))
        return best

    q0 = pl.multiple_of(h * 2 * PEER_KEY_HALF, 2 * PEER_KEY_HALF)
    qt = jnp.dot(wqt_ref[pl.ds(q0, 2 * PEER_KEY_HALF), :], xnt_scr[...],
                 preferred_element_type=jnp.float32).astype(jnp.bfloat16)
    for half in range(2):
        s = jnp.dot(keys_ref[2 * h + half], qt[half * PEER_KEY_HALF:(half + 1) * PEER_KEY_HALF, :],
                    preferred_element_type=jnp.float32)
        for g in range(ngroups):
            slab_scr[half, slab_rows(g), :] = s[:, g * LANES:(g + 1) * LANES]
    a = top16(0)
    b = top16(1)
    cmax = a[0] + b[0]
    rows = [[a[i] + b[jj] for jj in range(PEER_TOPK // (i + 1))] for i in range(PEER_TOPK)]
    cands = [(i, jj, rows[i][jj]) for i in range(PEER_TOPK) for jj in range(len(rows[i]))]
    best = rows[0]
    singles = [rows[i][0] for i in range(PEER_TOPK // 2, PEER_TOPK)]
    merge_rows = [rows[i] for i in range(1, PEER_TOPK // 2)] + [singles]
    for r, row in enumerate(merge_rows):
        best = _top16_select(best, row)
        if r + 1 < len(merge_rows):
            best = _bitonic_sort_desc(best)
    tau = functools.reduce(jnp.minimum, best)
    z = functools.reduce(lambda u, v: u + v, [jnp.exp(c - cmax) for c in best])
    inv_z = 0.5 / z
    e0s = [jnp.exp(a[i] - a[0]) * inv_z for i in range(PEER_TOPK)]
    e1s = [jnp.exp(b[jj] - b[0]) for jj in range(PEER_TOPK)]
    theta = None
    for i, jj, c in cands:
        pij = jnp.where(c >= tau, e0s[i] * e1s[jj], jnp.inf)
        theta = pij if theta is None else jnp.minimum(theta, pij)
    for g in range(ngroups):
        lanes = slice(g * LANES, (g + 1) * LANES)
        a0 = a[0][g:g + 1, :]
        b0 = b[0][g:g + 1, :]
        iz = inv_z[g:g + 1, :]
        e0 = jnp.exp(slab_scr[0, slab_rows(g), :] - a0) * iz
        for blk in range(N_KEYS // EXPERT_ROWS_I):
            e0_scr[h, blk, :, lanes] = e0[blk * EXPERT_ROWS_I:(blk + 1) * EXPERT_ROWS_I, :]
        e1_scr[h, :, lanes] = jnp.exp(slab_scr[1, slab_rows(g), :] - b0)
        theta_scr[h, :, lanes] = theta[g:g + 1, :]


def _peer_kernel(h1_ref, g2_ref, wqt_ref, keys_ref, u_first_ref, u_next0_ref, u_next1_ref,
                 vt_prev0_ref, vt_prev1_ref, vt_last_ref, gf_ref, out_ref,
                 xnt_scr, slab_scr, e0_scr, e1_scr, theta_scr, pre_scr, y_scr, acc_scr):
    e = pl.program_id(1)
    tt = xnt_scr.shape[1]
    u_next = (u_next0_ref, u_next1_ref)
    vt_prev = (vt_prev0_ref, vt_prev1_ref)
    assert EXPERT_ROWS_I == 2 * EXPERT_CHUNK_I

    @pl.when(e == 0)
    def _():
        for g in range(tt // LANES):
            rows = slice(g * LANES, (g + 1) * LANES)
            xn = _rms_norm_f32(h1_ref[rows, :], g2_ref[...])
            xnt_scr[:, rows] = xn.T.astype(jnp.bfloat16)

        def head_pair_body(hh, carry):
            for k in range(2):
                _route_tile(2 * hh + k, xnt_scr, wqt_ref, keys_ref, slab_scr.at[k], e0_scr, e1_scr,
                            theta_scr)
            return carry

        lax.fori_loop(0, PEER_HEADS // 2, head_pair_body, 0)
        acc_scr[...] = jnp.zeros_like(acc_scr)
        y_scr[1] = jnp.zeros(y_scr.shape[1:], y_scr.dtype)
        pre_scr[0] = jnp.dot(u_first_ref[...], xnt_scr[...], preferred_element_type=jnp.float32)

    for j in range(2):
        for ic in range(EXPERT_CHUNK_I):
            if ic == 1:
                pre_scr[(j + 1) % 2] = jnp.dot(u_next[j][...], xnt_scr[...],
                                               preferred_element_type=jnp.float32)
            if ic == 2:
                acc_scr[...] += jnp.dot(vt_prev[j][...], y_scr[(j + 1) % 2],
                                        preferred_element_type=jnp.float32)
            il = j * EXPERT_CHUNK_I + ic
            for tg in range(tt // LANES):
                lanes = slice(tg * LANES, (tg + 1) * LANES)
                e0v = [jnp.broadcast_to(e0_scr[h, e, il:il + 1, lanes], (SUBLANES, LANES))
                       for h in range(PEER_HEADS)]
                thv = [jnp.broadcast_to(theta_scr[h, :, lanes], (SUBLANES, LANES))
                       for h in range(PEER_HEADS)]
                for jp in range(N_KEYS // (2 * SUBLANES)):
                    ys = []
                    for half in range(2):
                        j0 = (2 * jp + half) * SUBLANES
                        gate = None
                        for h in range(PEER_HEADS):
                            p = e0v[h] * e1_scr[h, j0:j0 + SUBLANES, lanes]
                            sel = jnp.where(p >= thv[h], p, 0.0)
                            gate = sel if gate is None else gate + sel
                        a = pre_scr[j, ic * N_KEYS + j0:ic * N_KEYS + j0 + SUBLANES, lanes]
                        act = a * (1.0 + lax.erf(a * _SQRT_HALF))
                        ys.append(gate * act)
                    r0 = ic * N_KEYS + 2 * jp * SUBLANES
                    y_scr[j, r0:r0 + 2 * SUBLANES, lanes] = (
                        jnp.concatenate(ys, axis=0).astype(jnp.bfloat16))

    @pl.when(e == pl.num_programs(1) - 1)
    def _():
        acc_scr[...] += jnp.dot(vt_last_ref[...], y_scr[1], preferred_element_type=jnp.float32)
        for g in range(tt // LANES):
            rows = slice(g * LANES, (g + 1) * LANES)
            out_ref[rows, :] = _rms_norm_f32(h1_ref[rows, :] + acc_scr[:, rows].T, gf_ref[...])


def _peer(h1, g2, wqt, keys, u, vt, gf):
    t, d = h1.shape
    tt = PEER_TILE_TOKENS
    n_exp = u.shape[0]
    chunk_rows = EXPERT_CHUNK_I * N_KEYS
    n_chunks = n_exp // chunk_rows
    n_steps = N_KEYS // EXPERT_ROWS_I
    assert t % tt == 0 and tt == SUBLANES * LANES and n_chunks == 2 * n_steps
    ngroups = tt // LANES
    const = lambda shape: pl.BlockSpec(shape, lambda i, e: (0,) * len(shape),
                                       pipeline_mode=pl.Buffered(1))
    u_next = lambda j: pl.BlockSpec(
        (chunk_rows, d), lambda i, e: (jnp.minimum(2 * e + j + 1, n_chunks - 1), 0))
    vt_prev = lambda j: pl.BlockSpec(
        (None, d, chunk_rows), lambda i, e: (jnp.maximum(2 * e + j - 1, 0), 0, 0))
    return pl.pallas_call(
        _peer_kernel,
        grid=(t // tt, n_steps),
        in_specs=[
            pl.BlockSpec((tt, d), lambda i, e: (i, 0)),
            const(g2.shape), const(wqt.shape), const(keys.shape),
            pl.BlockSpec((chunk_rows, d), lambda i, e: (0, 0), pipeline_mode=pl.Buffered(1)),
            u_next(0), u_next(1), vt_prev(0), vt_prev(1),
            pl.BlockSpec((None, d, chunk_rows), lambda i, e: (n_chunks - 1, 0, 0),
                         pipeline_mode=pl.Buffered(1)),
            const(gf.shape),
        ],
        out_specs=pl.BlockSpec((tt, d), lambda i, e: (i, 0), pipeline_mode=pl.Buffered(1)),
        out_shape=jax.ShapeDtypeStruct((t, d), jnp.float32),
        scratch_shapes=[
            pltpu.VMEM((d, tt), jnp.bfloat16),
            pltpu.VMEM((2, 2, ngroups * GROUP_PITCH, LANES), jnp.float32),
            pltpu.VMEM((PEER_HEADS, n_steps, EXPERT_ROWS_I, tt), jnp.float32),
            pltpu.VMEM((PEER_HEADS, N_KEYS, tt), jnp.float32),
            pltpu.VMEM((PEER_HEADS, 1, tt), jnp.float32),
            pltpu.VMEM((2, chunk_rows, tt), jnp.float32),
            pltpu.VMEM((2, chunk_rows, tt), jnp.bfloat16),
            pltpu.VMEM((d, tt), jnp.float32),
        ],
        compiler_params=pltpu.CompilerParams(
            dimension_semantics=("arbitrary", "arbitrary"), vmem_limit_bytes=VMEM_LIMIT_BYTES),
        name="peer",
    )(h1, g2, wqt, keys, u, u, u, vt, vt, vt, gf)


def kernel(x, meta_tokens, norm1_g, w_in, conv_w, conv_b, pool_w, pool_scale, w_out, norm2_g,
           peer_w_q, peer_keys, peer_u, peer_v, final_norm_g):
    b, seq, d = x.shape
    bf = jnp.bfloat16
    h1 = _mixer(x, meta_tokens, norm1_g.reshape(1, d), w_in.astype(bf), conv_w,
                conv_b.reshape(1, -1), pool_w.astype(bf), pool_scale.reshape(1, -1), w_out.astype(bf))
    keys = peer_keys.reshape(2 * PEER_HEADS, N_KEYS, PEER_KEY_HALF).astype(bf)
    chunk_rows = EXPERT_CHUNK_I * N_KEYS
    vt = peer_v.astype(bf).reshape(-1, chunk_rows, d).transpose(0, 2, 1)
    out = _peer(h1.reshape(b * seq, d), norm2_g.reshape(1, d), peer_w_q.T.astype(bf), keys,
                peer_u.astype(bf), vt, final_norm_g.reshape(1, d))
    return out.reshape(b, seq, d)
```

```python
import functools
import math

import jax
import jax.numpy as jnp
from jax import lax
from jax.experimental import pallas as pl
from jax.experimental.pallas import tpu as pltpu

N_META = 16
EPS = 1e-6
CONV_CH = 512
POOL_WINDOWS = (2, 4, 8, 16)
POOL_GROUP_DIM = 128
PEER_HEADS = 8
N_KEYS = 128
PEER_KEY_HALF = 128
PEER_TOPK = 16

LANES = 128
SUBLANES = 8
VMEM_LIMIT_BYTES = 56 * 1024 * 1024

HALO = 16
MIX_TILE = 1024
ROUTE_TILE = SUBLANES * LANES
GROUP_PITCH = N_KEYS + SUBLANES
EXPERT_TILE_TOKENS = 512
EXPERT_ROWS_I = 16
EXPERT_CHUNK_I = 16

_SQRT_HALF = math.sqrt(0.5)


def _rms_norm_f32(v, g):
    return v * lax.rsqrt(jnp.mean(v * v, axis=-1, keepdims=True) + EPS) * g


def _mixer_kernel(seq_len, xprev_ref, x_ref, xnext_ref, meta_ref, g1_ref, win_ref, convw_ref,
                  convb_ref, poolw_ref, pscale_ref, wout_ref, h1_ref, hn_scr, p_scr, z_scr, y_scr):
    ts = x_ref.shape[1]
    j = pl.program_id(1)
    last = pl.num_programs(1) - 1
    g1 = g1_ref[...]

    x = x_ref[0]
    prev = jnp.where(j == 0, meta_ref[...], xprev_ref[0])
    nxt = jnp.where(j == last, 0.0, _rms_norm_f32(xnext_ref[0], g1))
    hn_scr[0:HALO, :] = _rms_norm_f32(prev, g1).astype(jnp.bfloat16)
    hn_scr[HALO:HALO + ts, :] = _rms_norm_f32(x, g1).astype(jnp.bfloat16)
    hn_scr[HALO + ts:HALO + ts + HALO, :] = nxt.astype(jnp.bfloat16)

    p_scr[...] = jnp.dot(hn_scr[...], win_ref[...], preferred_element_type=jnp.float32)

    z_scr[...] = p_scr[:, 2 * CONV_CH:3 * CONV_CH] * p_scr[:, 0:CONV_CH]
    conv = (z_scr[HALO - 1:HALO - 1 + ts, :] * convw_ref[0:1, :]
            + z_scr[HALO:HALO + ts, :] * convw_ref[1:2, :]
            + z_scr[HALO + 1:HALO + 1 + ts, :] * convw_ref[2:3, :]
            + convb_ref[...])
    ya = p_scr[HALO:HALO + ts, CONV_CH:2 * CONV_CH] * conv
    y_scr[:, 0:CONV_CH] = ya.astype(jnp.bfloat16)

    pos = j * ts + lax.broadcasted_iota(jnp.int32, (ts, 1), 0)
    hb0 = 3 * CONV_CH
    for g, w in enumerate(POOL_WINDOWS):
        c0 = hb0 + g * POOL_GROUP_DIM
        c1 = c0 + POOL_GROUP_DIM
        acc = p_scr[HALO - w // 2:HALO - w // 2 + ts, c0:c1]
        for k in range(-w // 2 + 1, w // 2):
            acc = acc + p_scr[HALO + k:HALO + k + ts, c0:c1]
        cnt = (w - jnp.maximum(pos + w // 2 - seq_len, 0)).astype(jnp.float32)
        mixed = acc / cnt - p_scr[HALO:HALO + ts, c0:c1]
        yb = jnp.dot(mixed.astype(jnp.bfloat16), poolw_ref[g], preferred_element_type=jnp.float32)
        yb = yb * pscale_ref[:, g * POOL_GROUP_DIM:(g + 1) * POOL_GROUP_DIM]
        y_scr[:, CONV_CH + g * POOL_GROUP_DIM:CONV_CH + (g + 1) * POOL_GROUP_DIM] = yb.astype(jnp.bfloat16)

    h1_ref[0] = x + jnp.dot(y_scr[...], wout_ref[...], preferred_element_type=jnp.float32)


def _mixer(x, meta, g1, win, convw, convb, poolw, pscale, wout):
    b, seq, d = x.shape
    ts = MIX_TILE
    assert seq % ts == 0 and ts % HALO == 0 and meta.shape[0] == HALO
    nt = seq // ts
    hb = ts // HALO
    nhb = seq // HALO
    rows = ts + 2 * HALO
    full = lambda shape: pl.BlockSpec(shape, lambda bi, j: (0,) * len(shape),
                                      pipeline_mode=pl.Buffered(1))
    return pl.pallas_call(
        functools.partial(_mixer_kernel, seq),
        grid=(b, nt),
        in_specs=[
            pl.BlockSpec((1, HALO, d), lambda bi, j: (bi, jnp.maximum(j * hb - 1, 0), 0)),
            pl.BlockSpec((1, ts, d), lambda bi, j: (bi, j, 0)),
            pl.BlockSpec((1, HALO, d), lambda bi, j: (bi, jnp.minimum((j + 1) * hb, nhb - 1), 0)),
            full(meta.shape), full(g1.shape), full(win.shape), full(convw.shape), full(convb.shape),
            full(poolw.shape), full(pscale.shape), full(wout.shape),
        ],
        out_specs=pl.BlockSpec((1, ts, d), lambda bi, j: (bi, j, 0)),
        out_shape=jax.ShapeDtypeStruct((b, seq, d), jnp.float32),
        scratch_shapes=[
            pltpu.VMEM((rows, d), jnp.bfloat16),
            pltpu.VMEM((rows, win.shape[1]), jnp.float32),
            pltpu.VMEM((rows, CONV_CH), jnp.float32),
            pltpu.VMEM((ts, d), jnp.bfloat16),
        ],
        compiler_params=pltpu.CompilerParams(
            dimension_semantics=("arbitrary", "arbitrary"), vmem_limit_bytes=VMEM_LIMIT_BYTES),
        name="mixer",
    )(x, x, x, meta, g1, win, convw, convb, poolw, pscale, wout)


def _cmpx(v, a, b):
    hi = jnp.maximum(v[a], v[b])
    lo = jnp.minimum(v[a], v[b])
    v[a], v[b] = hi, lo


def _oddeven_merge_sort_pairs(n):
    pairs = []
    p = 1
    while p < n:
        k = p
        while k >= 1:
            for jj in range(k % p, n - k, 2 * k):
                for i in range(min(k, n - jj - k)):
                    if (i + jj) // (2 * p) == (i + jj + k) // (2 * p):
                        pairs.append((i + jj, i + jj + k))
            k //= 2
        p *= 2
    return pairs


_SORT16 = _oddeven_merge_sort_pairs(PEER_TOPK)


def _sort_desc(v):
    v = list(v)
    for a, b in _SORT16:
        _cmpx(v, a, b)
    return v


def _bitonic_sort_desc(v):
    v = list(v)
    d = len(v) // 2
    while d >= 1:
        for i in range(len(v)):
            if i & d == 0:
                _cmpx(v, i, i + d)
        d //= 2
    return v


def _top16_select(xs, ys):
    n = len(xs)
    out = list(xs)
    for m, y in enumerate(ys):
        out[n - 1 - m] = jnp.maximum(out[n - 1 - m], y)
    return out


def _routing_kernel(h1_ref, g2_ref, wqt_ref, keys_ref, xnt_ref, e0_ref, e1_ref, theta_ref,
                    xnb_scr, slab_scr):
    tr = h1_ref.shape[0]
    ngroups = tr // LANES
    xn = _rms_norm_f32(h1_ref[...], g2_ref[...])
    xnb_scr[...] = xn.astype(jnp.bfloat16)
    xnt_ref[...] = xn.T.astype(jnp.bfloat16)

    def slab_rows(g):
        return pl.ds(g * GROUP_PITCH, N_KEYS)

    def load_key(hp, n):
        return slab_scr[hp, pl.ds(n, ngroups, stride=GROUP_PITCH), :]

    def top16(hp):
        best = None
        for blk in range(N_KEYS // PEER_TOPK):
            cur = _sort_desc([load_key(hp, blk * PEER_TOPK + m) for m in range(PEER_TOPK)])
            best = cur if best is None else _bitonic_sort_desc(_top16_select(best, cur))
        return best

    def head_body(h, carry):
        q0 = pl.multiple_of(h * 2 * PEER_KEY_HALF, 2 * PEER_KEY_HALF)
        qt = lax.dot_general(wqt_ref[pl.ds(q0, 2 * PEER_KEY_HALF), :], xnb_scr[...],
                             (((1,), (1,)), ((), ())),
                             preferred_element_type=jnp.float32).astype(jnp.bfloat16)
        for half in range(2):
            s = jnp.dot(keys_ref[2 * h + half], qt[half * PEER_KEY_HALF:(half + 1) * PEER_KEY_HALF, :],
                        preferred_element_type=jnp.float32)
            for g in range(ngroups):
                slab_scr[half, slab_rows(g), :] = s[:, g * LANES:(g + 1) * LANES]
        a = top16(0)
        b = top16(1)
        cmax = a[0] + b[0]
        rows = [[a[i] + b[jj] for jj in range(PEER_TOPK // (i + 1))] for i in range(PEER_TOPK)]
        cands = [(i, jj, rows[i][jj]) for i in range(PEER_TOPK) for jj in range(len(rows[i]))]
        best = rows[0]
        singles = [rows[i][0] for i in range(PEER_TOPK // 2, PEER_TOPK)]
        merge_rows = [rows[i] for i in range(1, PEER_TOPK // 2)] + [singles]
        for r, row in enumerate(merge_rows):
            best = _top16_select(best, row)
            if r + 1 < len(merge_rows):
                best = _bitonic_sort_desc(best)
        tau = functools.reduce(jnp.minimum, best)
        z = functools.reduce(lambda u, v: u + v, [jnp.exp(c - cmax) for c in best])
        inv_z = 0.5 / z
        e0s = [jnp.exp(a[i] - a[0]) * inv_z for i in range(PEER_TOPK)]
        e1s = [jnp.exp(b[jj] - b[0]) for jj in range(PEER_TOPK)]
        theta = None
        for i, jj, c in cands:
            pij = jnp.where(c >= tau, e0s[i] * e1s[jj], jnp.inf)
            theta = pij if theta is None else jnp.minimum(theta, pij)
        for g in range(ngroups):
            lanes = slice(g * LANES, (g + 1) * LANES)
            a0 = a[0][g:g + 1, :]
            b0 = b[0][g:g + 1, :]
            iz = inv_z[g:g + 1, :]
            e0_ref[h, :, lanes] = jnp.exp(slab_scr[0, slab_rows(g), :] - a0) * iz
            e1_ref[h, :, lanes] = jnp.exp(slab_scr[1, slab_rows(g), :] - b0)
            theta_ref[h, :, lanes] = theta[g:g + 1, :]
        return carry

    lax.fori_loop(0, PEER_HEADS, head_body, 0)


def _routing(h1, g2, wqt, keys):
    t, d = h1.shape
    tr = ROUTE_TILE
    assert t % tr == 0
    ngroups = tr // LANES
    full = lambda shape: pl.BlockSpec(shape, lambda i: (0,) * len(shape),
                                      pipeline_mode=pl.Buffered(1))
    return pl.pallas_call(
        _routing_kernel,
        grid=(t // tr,),
        in_specs=[pl.BlockSpec((tr, d), lambda i: (i, 0)), full(g2.shape), full(wqt.shape),
                  full(keys.shape)],
        out_specs=[
            pl.BlockSpec((d, tr), lambda i: (0, i)),
            pl.BlockSpec((PEER_HEADS, N_KEYS, tr), lambda i: (0, 0, i)),
            pl.BlockSpec((PEER_HEADS, N_KEYS, tr), lambda i: (0, 0, i)),
            pl.BlockSpec((PEER_HEADS, 1, tr), lambda i: (0, 0, i)),
        ],
        out_shape=[
            jax.ShapeDtypeStruct((d, t), jnp.bfloat16),
            jax.ShapeDtypeStruct((PEER_HEADS, N_KEYS, t), jnp.float32),
            jax.ShapeDtypeStruct((PEER_HEADS, N_KEYS, t), jnp.float32),
            jax.ShapeDtypeStruct((PEER_HEADS, 1, t), jnp.float32),
        ],
        scratch_shapes=[
            pltpu.VMEM((tr, d), jnp.bfloat16),
            pltpu.VMEM((2, ngroups * GROUP_PITCH, LANES), jnp.float32),
        ],
        compiler_params=pltpu.CompilerParams(
            dimension_semantics=("arbitrary",), vmem_limit_bytes=VMEM_LIMIT_BYTES),
        name="routing",
    )(h1, g2, wqt, keys)


def _expert_kernel(xnt_ref, u_ref, vt_ref, e0_ref, e1_ref, theta_ref, h1_ref, gf_ref, out_ref,
                   pre_scr, y_scr, acc_scr):
    e = pl.program_id(1)
    tt = xnt_ref.shape[1]

    @pl.when(e == 0)
    def _():
        acc_scr[...] = jnp.zeros_like(acc_scr)

    n_chunks = EXPERT_ROWS_I // EXPERT_CHUNK_I
    chunk_rows = EXPERT_CHUNK_I * N_KEYS

    def pre_activations(c):
        pre_scr[c % 2] = jnp.dot(u_ref[c * chunk_rows:(c + 1) * chunk_rows, :], xnt_ref[...],
                                 preferred_element_type=jnp.float32)

    def output_part(c):
        c0 = c * chunk_rows
        acc_scr[...] += jnp.dot(vt_ref[:, c0:c0 + chunk_rows], y_scr[c0:c0 + chunk_rows, :],
                                preferred_element_type=jnp.float32)

    pre_activations(0)
    for c in range(n_chunks + 1):
        for ic in range(EXPERT_CHUNK_I):
            if ic == 1 and c + 1 < n_chunks:
                pre_activations(c + 1)
            if ic == 2 and c >= 1:
                output_part(c - 1)
            if c == n_chunks:
                continue
            il = c * EXPERT_CHUNK_I + ic
            for tg in range(tt // LANES):
                lanes = slice(tg * LANES, (tg + 1) * LANES)
                e0v = [jnp.broadcast_to(e0_ref[h, il:il + 1, lanes], (SUBLANES, LANES))
                       for h in range(PEER_HEADS)]
                thv = [jnp.broadcast_to(theta_ref[h, :, lanes], (SUBLANES, LANES))
                       for h in range(PEER_HEADS)]
                for jp in range(N_KEYS // (2 * SUBLANES)):
                    ys = []
                    for half in range(2):
                        j0 = (2 * jp + half) * SUBLANES
                        gate = None
                        for h in range(PEER_HEADS):
                            p = e0v[h] * e1_ref[h, j0:j0 + SUBLANES, lanes]
                            sel = jnp.where(p >= thv[h], p, 0.0)
                            gate = sel if gate is None else gate + sel
                        a = pre_scr[c % 2, ic * N_KEYS + j0:ic * N_KEYS + j0 + SUBLANES, lanes]
                        act = a * (1.0 + lax.erf(a * _SQRT_HALF))
                        ys.append(gate * act)
                    r0 = il * N_KEYS + 2 * jp * SUBLANES
                    y_scr[r0:r0 + 2 * SUBLANES, lanes] = (
                        jnp.concatenate(ys, axis=0).astype(jnp.bfloat16))

    @pl.when(e == pl.num_programs(1) - 1)
    def _():
        h2 = h1_ref[...] + acc_scr[...].T
        out_ref[...] = _rms_norm_f32(h2, gf_ref[...])


def _experts(xnt, u, vt, e0, e1, theta, h1, gf):
    d, t = xnt.shape
    tt = EXPERT_TILE_TOKENS
    rows = EXPERT_ROWS_I * N_KEYS
    n_exp = u.shape[0]
    assert t % tt == 0 and n_exp % rows == 0
    return pl.pallas_call(
        _expert_kernel,
        grid=(t // tt, n_exp // rows),
        in_specs=[
            pl.BlockSpec((d, tt), lambda i, e: (0, i)),
            pl.BlockSpec((rows, d), lambda i, e: (e, 0)),
            pl.BlockSpec((None, d, rows), lambda i, e: (e, 0, 0)),
            pl.BlockSpec((PEER_HEADS, EXPERT_ROWS_I, tt), lambda i, e: (0, e, i)),
            pl.BlockSpec((PEER_HEADS, N_KEYS, tt), lambda i, e: (0, 0, i)),
            pl.BlockSpec((PEER_HEADS, 1, tt), lambda i, e: (0, 0, i)),
            pl.BlockSpec((tt, d), lambda i, e: (i, 0)),
            pl.BlockSpec(gf.shape, lambda i, e: (0, 0)),
        ],
        out_specs=pl.BlockSpec((tt, d), lambda i, e: (i, 0)),
        out_shape=jax.ShapeDtypeStruct((t, d), jnp.float32),
        scratch_shapes=[
            pltpu.VMEM((2, EXPERT_CHUNK_I * N_KEYS, tt), jnp.float32),
            pltpu.VMEM((rows, tt), jnp.bfloat16),
            pltpu.VMEM((d, tt), jnp.float32),
        ],
        compiler_params=pltpu.CompilerParams(
            dimension_semantics=("arbitrary", "arbitrary"), vmem_limit_bytes=VMEM_LIMIT_BYTES),
        name="experts",
    )(xnt, u, vt, e0, e1, theta, h1, gf)


def kernel(x, meta_tokens, norm1_g, w_in, conv_w, conv_b, pool_w, pool_scale, w_out, norm2_g,
           peer_w_q, peer_keys, peer_u, peer_v, final_norm_g):
    b, seq, d = x.shape
    bf = jnp.bfloat16
    h1 = _mixer(x, meta_tokens, norm1_g.reshape(1, d), w_in.astype(bf), conv_w,
                conv_b.reshape(1, -1), pool_w.astype(bf), pool_scale.reshape(1, -1), w_out.astype(bf))
    h1 = h1.reshape(b * seq, d)
    keys = peer_keys.reshape(2 * PEER_HEADS, N_KEYS, PEER_KEY_HALF).astype(bf)
    xnt, e0, e1, theta = _routing(h1, norm2_g.reshape(1, d), peer_w_q.T.astype(bf), keys)
    vt = peer_v.astype(bf).reshape(-1, EXPERT_ROWS_I * N_KEYS, d).transpose(0, 2, 1)
    out = _experts(xnt, peer_u.astype(bf), vt, e0, e1, theta, h1, final_norm_g.reshape(1, d))
    return out.reshape(b, seq, d)
```

```python
import functools
import math

import jax
import jax.numpy as jnp
from jax import lax
from jax.experimental import pallas as pl
from jax.experimental.pallas import tpu as pltpu

N_META = 16
EPS = 1e-6
CONV_CH = 512
POOL_WINDOWS = (2, 4, 8, 16)
POOL_GROUP_DIM = 128
PEER_HEADS = 8
N_KEYS = 128
PEER_KEY_HALF = 128
PEER_TOPK = 16

LANES = 128
SUBLANES = 8
VMEM_LIMIT_BYTES = 56 * 1024 * 1024

HALO = 16
MIX_TILE = 1024
ROUTE_TILE = SUBLANES * LANES
GROUP_PITCH = N_KEYS + SUBLANES
EXPERT_TILE_TOKENS = 512
EXPERT_ROWS_I = 16
EXPERT_CHUNK_I = 16

_SQRT_HALF = math.sqrt(0.5)


def _rms_norm_f32(v, g):
    return v * lax.rsqrt(jnp.mean(v * v, axis=-1, keepdims=True) + EPS) * g


def _mixer_kernel(seq_len, xprev_ref, x_ref, xnext_ref, meta_ref, g1_ref, win_ref, convw_ref,
                  convb_ref, poolw_ref, pscale_ref, wout_ref, h1_ref, hn_scr, p_scr, z_scr, y_scr):
    ts = x_ref.shape[1]
    j = pl.program_id(1)
    last = pl.num_programs(1) - 1
    g1 = g1_ref[...]

    x = x_ref[0]
    prev = jnp.where(j == 0, meta_ref[...], xprev_ref[0])
    nxt = jnp.where(j == last, 0.0, _rms_norm_f32(xnext_ref[0], g1))
    hn_scr[0:HALO, :] = _rms_norm_f32(prev, g1).astype(jnp.bfloat16)
    hn_scr[HALO:HALO + ts, :] = _rms_norm_f32(x, g1).astype(jnp.bfloat16)
    hn_scr[HALO + ts:HALO + ts + HALO, :] = nxt.astype(jnp.bfloat16)

    p_scr[...] = jnp.dot(hn_scr[...], win_ref[...], preferred_element_type=jnp.float32)

    z_scr[...] = p_scr[:, 2 * CONV_CH:3 * CONV_CH] * p_scr[:, 0:CONV_CH]
    conv = (z_scr[HALO - 1:HALO - 1 + ts, :] * convw_ref[0:1, :]
            + z_scr[HALO:HALO + ts, :] * convw_ref[1:2, :]
            + z_scr[HALO + 1:HALO + 1 + ts, :] * convw_ref[2:3, :]
            + convb_ref[...])
    ya = p_scr[HALO:HALO + ts, CONV_CH:2 * CONV_CH] * conv
    y_scr[:, 0:CONV_CH] = ya.astype(jnp.bfloat16)

    pos = j * ts + lax.broadcasted_iota(jnp.int32, (ts, 1), 0)
    hb0 = 3 * CONV_CH
    for g, w in enumerate(POOL_WINDOWS):
        c0 = hb0 + g * POOL_GROUP_DIM
        c1 = c0 + POOL_GROUP_DIM
        acc = p_scr[HALO - w // 2:HALO - w // 2 + ts, c0:c1]
        for k in range(-w // 2 + 1, w // 2):
            acc = acc + p_scr[HALO + k:HALO + k + ts, c0:c1]
        cnt = (w - jnp.maximum(pos + w // 2 - seq_len, 0)).astype(jnp.float32)
        mixed = acc / cnt - p_scr[HALO:HALO + ts, c0:c1]
        yb = jnp.dot(mixed.astype(jnp.bfloat16), poolw_ref[g], preferred_element_type=jnp.float32)
        yb = yb * pscale_ref[:, g * POOL_GROUP_DIM:(g + 1) * POOL_GROUP_DIM]
        y_scr[:, CONV_CH + g * POOL_GROUP_DIM:CONV_CH + (g + 1) * POOL_GROUP_DIM] = yb.astype(jnp.bfloat16)

    h1_ref[0] = x + jnp.dot(y_scr[...], wout_ref[...], preferred_element_type=jnp.float32)


def _mixer(x, meta, g1, win, convw, convb, poolw, pscale, wout):
    b, seq, d = x.shape
    ts = MIX_TILE
    assert seq % ts == 0 and ts % HALO == 0 and meta.shape[0] == HALO
    nt = seq // ts
    hb = ts // HALO
    nhb = seq // HALO
    rows = ts + 2 * HALO
    full = lambda shape: pl.BlockSpec(shape, lambda bi, j: (0,) * len(shape),
                                      pipeline_mode=pl.Buffered(1))
    return pl.pallas_call(
        functools.partial(_mixer_kernel, seq),
        grid=(b, nt),
        in_specs=[
            pl.BlockSpec((1, HALO, d), lambda bi, j: (bi, jnp.maximum(j * hb - 1, 0), 0)),
            pl.BlockSpec((1, ts, d), lambda bi, j: (bi, j, 0)),
            pl.BlockSpec((1, HALO, d), lambda bi, j: (bi, jnp.minimum((j + 1) * hb, nhb - 1), 0)),
            full(meta.shape), full(g1.shape), full(win.shape), full(convw.shape), full(convb.shape),
            full(poolw.shape), full(pscale.shape), full(wout.shape),
        ],
        out_specs=pl.BlockSpec((1, ts, d), lambda bi, j: (bi, j, 0)),
        out_shape=jax.ShapeDtypeStruct((b, seq, d), jnp.float32),
        scratch_shapes=[
            pltpu.VMEM((rows, d), jnp.bfloat16),
            pltpu.VMEM((rows, win.shape[1]), jnp.float32),
            pltpu.VMEM((rows, CONV_CH), jnp.float32),
            pltpu.VMEM((ts, d), jnp.bfloat16),
        ],
        compiler_params=pltpu.CompilerParams(
            dimension_semantics=("arbitrary", "arbitrary"), vmem_limit_bytes=VMEM_LIMIT_BYTES),
        name="mixer",
    )(x, x, x, meta, g1, win, convw, convb, poolw, pscale, wout)


def _cmpx(v, a, b):
    hi = jnp.maximum(v[a], v[b])
    lo = jnp.minimum(v[a], v[b])
    v[a], v[b] = hi, lo


def _oddeven_merge_sort_pairs(n):
    pairs = []
    p = 1
    while p < n:
        k = p
        while k >= 1:
            for jj in range(k % p, n - k, 2 * k):
                for i in range(min(k, n - jj - k)):
                    if (i + jj) // (2 * p) == (i + jj + k) // (2 * p):
                        pairs.append((i + jj, i + jj + k))
            k //= 2
        p *= 2
    return pairs


_SORT16 = _oddeven_merge_sort_pairs(PEER_TOPK)


def _sort_desc(v):
    v = list(v)
    for a, b in _SORT16:
        _cmpx(v, a, b)
    return v


def _bitonic_sort_desc(v):
    v = list(v)
    d = len(v) // 2
    while d >= 1:
        for i in range(len(v)):
            if i & d == 0:
                _cmpx(v, i, i + d)
        d //= 2
    return v


def _top16_select(xs, ys):
    n = len(xs)
    out = list(xs)
    for m, y in enumerate(ys):
        out[n - 1 - m] = jnp.maximum(out[n - 1 - m], y)
    return out


def _routing_kernel(h1_ref, g2_ref, wqt_ref, keys_ref, xnt_ref, e0_ref, e1_ref, theta_ref,
                    xnb_scr, slab_scr):
    tr = h1_ref.shape[0]
    ngroups = tr // LANES
    xn = _rms_norm_f32(h1_ref[...], g2_ref[...])
    xnb_scr[...] = xn.astype(jnp.bfloat16)
    xnt_ref[...] = xn.T.astype(jnp.bfloat16)

    def slab_rows(g):
        return pl.ds(g * GROUP_PITCH, N_KEYS)

    def load_key(hp, n):
        return slab_scr[hp, pl.ds(n, ngroups, stride=GROUP_PITCH), :]

    def top16(hp):
        best = None
        for blk in range(N_KEYS // PEER_TOPK):
            cur = _sort_desc([load_key(hp, blk * PEER_TOPK + m) for m in range(PEER_TOPK)])
            best = cur if best is None else _bitonic_sort_desc(_top16_select(best, cur))
        return best

    def head_body(h, carry):
        q0 = pl.multiple_of(h * 2 * PEER_KEY_HALF, 2 * PEER_KEY_HALF)
        qt = lax.dot_general(wqt_ref[pl.ds(q0, 2 * PEER_KEY_HALF), :], xnb_scr[...],
                             (((1,), (1,)), ((), ())),
                             preferred_element_type=jnp.float32).astype(jnp.bfloat16)
        for half in range(2):
            s = jnp.dot(keys_ref[2 * h + half], qt[half * PEER_KEY_HALF:(half + 1) * PEER_KEY_HALF, :],
                        preferred_element_type=jnp.float32)
            for g in range(ngroups):
                slab_scr[half, slab_rows(g), :] = s[:, g * LANES:(g + 1) * LANES]
        a = top16(0)
        b = top16(1)
        cmax = a[0] + b[0]
        rows = [[a[i] + b[jj] for jj in range(PEER_TOPK // (i + 1))] for i in range(PEER_TOPK)]
        cands = [(i, jj, rows[i][jj]) for i in range(PEER_TOPK) for jj in range(len(rows[i]))]
        best = rows[0]
        singles = [rows[i][0] for i in range(PEER_TOPK // 2, PEER_TOPK)]
        merge_rows = [rows[i] for i in range(1, PEER_TOPK // 2)] + [singles]
        for r, row in enumerate(merge_rows):
            best = _top16_select(best, row)
            if r + 1 < len(merge_rows):
                best = _bitonic_sort_desc(best)
        tau = functools.reduce(jnp.minimum, best)
        z = functools.reduce(lambda u, v: u + v, [jnp.exp(c - cmax) for c in best])
        inv_z = 0.5 / z
        e0s = [jnp.exp(a[i] - a[0]) * inv_z for i in range(PEER_TOPK)]
        e1s = [jnp.exp(b[jj] - b[0]) for jj in range(PEER_TOPK)]
        theta = None
        for i, jj, c in cands:
            pij = jnp.where(c >= tau, e0s[i] * e1s[jj], jnp.inf)
            theta = pij if theta is None else jnp.minimum(theta, pij)
        for g in range(ngroups):
            lanes = slice(g * LANES, (g + 1) * LANES)
            a0 = a[0][g:g + 1, :]
            b0 = b[0][g:g + 1, :]
            iz = inv_z[g:g + 1, :]
            e0_ref[h, :, lanes] = jnp.exp(slab_scr[0, slab_rows(g), :] - a0) * iz
            e1_ref[h, :, lanes] = jnp.exp(slab_scr[1, slab_rows(g), :] - b0)
            theta_ref[h, :, lanes] = theta[g:g + 1, :]
        return carry

    lax.fori_loop(0, PEER_HEADS, head_body, 0)


def _routing(h1, g2, wqt, keys):
    t, d = h1.shape
    tr = ROUTE_TILE
    assert t % tr == 0
    ngroups = tr // LANES
    full = lambda shape: pl.BlockSpec(shape, lambda i: (0,) * len(shape),
                                      pipeline_mode=pl.Buffered(1))
    return pl.pallas_call(
        _routing_kernel,
        grid=(t // tr,),
        in_specs=[pl.BlockSpec((tr, d), lambda i: (i, 0)), full(g2.shape), full(wqt.shape),
                  full(keys.shape)],
        out_specs=[
            pl.BlockSpec((d, tr), lambda i: (0, i)),
            pl.BlockSpec((PEER_HEADS, N_KEYS, tr), lambda i: (0, 0, i)),
            pl.BlockSpec((PEER_HEADS, N_KEYS, tr), lambda i: (0, 0, i)),
            pl.BlockSpec((PEER_HEADS, 1, tr), lambda i: (0, 0, i)),
        ],
        out_shape=[
            jax.ShapeDtypeStruct((d, t), jnp.bfloat16),
            jax.ShapeDtypeStruct((PEER_HEADS, N_KEYS, t), jnp.float32),
            jax.ShapeDtypeStruct((PEER_HEADS, N_KEYS, t), jnp.float32),
            jax.ShapeDtypeStruct((PEER_HEADS, 1, t), jnp.float32),
        ],
        scratch_shapes=[
            pltpu.VMEM((tr, d), jnp.bfloat16),
            pltpu.VMEM((2, ngroups * GROUP_PITCH, LANES), jnp.float32),
        ],
        compiler_params=pltpu.CompilerParams(
            dimension_semantics=("arbitrary",), vmem_limit_bytes=VMEM_LIMIT_BYTES),
        name="routing",
    )(h1, g2, wqt, keys)


def _expert_kernel(xnt_ref, u_ref, vt_ref, e0_ref, e1_ref, theta_ref, h1_ref, gf_ref, out_ref,
                   pre_scr, y_scr, acc_scr, e0b_scr, thb_scr):
    e = pl.program_id(1)
    tt = xnt_ref.shape[1]

    @pl.when(e == 0)
    def _():
        acc_scr[...] = jnp.zeros_like(acc_scr)

    n_chunks = EXPERT_ROWS_I // EXPERT_CHUNK_I
    chunk_rows = EXPERT_CHUNK_I * N_KEYS

    def pre_activations(c):
        pre_scr[c % 2] = jnp.dot(u_ref[c * chunk_rows:(c + 1) * chunk_rows, :], xnt_ref[...],
                                 preferred_element_type=jnp.float32)

    def output_part(c):
        c0 = c * chunk_rows
        acc_scr[...] += jnp.dot(vt_ref[:, c0:c0 + chunk_rows], y_scr[c0:c0 + chunk_rows, :],
                                preferred_element_type=jnp.float32)

    def broadcast_rows(h, carry):
        for tg in range(tt // LANES):
            lanes = slice(tg * LANES, (tg + 1) * LANES)
            thb_scr[h, tg] = jnp.broadcast_to(theta_ref[h, :, lanes], (SUBLANES, LANES))
            for il in range(EXPERT_ROWS_I):
                e0b_scr[h, il, tg] = jnp.broadcast_to(e0_ref[h, il:il + 1, lanes], (SUBLANES, LANES))
        return carry

    lax.fori_loop(0, PEER_HEADS, broadcast_rows, 0)

    pre_activations(0)
    for c in range(n_chunks + 1):
        for ic in range(EXPERT_CHUNK_I):
            if ic == 1 and c + 1 < n_chunks:
                pre_activations(c + 1)
            if ic == 2 and c >= 1:
                output_part(c - 1)
            if c == n_chunks:
                continue
            il = c * EXPERT_CHUNK_I + ic
            for tg in range(tt // LANES):
                lanes = slice(tg * LANES, (tg + 1) * LANES)
                e0v = [e0b_scr[h, il, tg] for h in range(PEER_HEADS)]
                thv = [thb_scr[h, tg] for h in range(PEER_HEADS)]
                for jp in range(N_KEYS // (2 * SUBLANES)):
                    ys = []
                    for half in range(2):
                        j0 = (2 * jp + half) * SUBLANES
                        gate = None
                        for h in range(PEER_HEADS):
                            p = e0v[h] * e1_ref[h, j0:j0 + SUBLANES, lanes]
                            sel = jnp.where(p >= thv[h], p, 0.0)
                            gate = sel if gate is None else gate + sel
                        a = pre_scr[c % 2, ic * N_KEYS + j0:ic * N_KEYS + j0 + SUBLANES, lanes]
                        act = a * (1.0 + lax.erf(a * _SQRT_HALF))
                        ys.append(gate * act)
                    r0 = il * N_KEYS + 2 * jp * SUBLANES
                    y_scr[r0:r0 + 2 * SUBLANES, lanes] = (
                        jnp.concatenate(ys, axis=0).astype(jnp.bfloat16))

    @pl.when(e == pl.num_programs(1) - 1)
    def _():
        h2 = h1_ref[...] + acc_scr[...].T
        out_ref[...] = _rms_norm_f32(h2, gf_ref[...])


def _experts(xnt, u, vt, e0, e1, theta, h1, gf):
    d, t = xnt.shape
    tt = EXPERT_TILE_TOKENS
    rows = EXPERT_ROWS_I * N_KEYS
    n_exp = u.shape[0]
    assert t % tt == 0 and n_exp % rows == 0
    return pl.pallas_call(
        _expert_kernel,
        grid=(t // tt, n_exp // rows),
        in_specs=[
            pl.BlockSpec((d, tt), lambda i, e: (0, i)),
            pl.BlockSpec((rows, d), lambda i, e: (e, 0)),
            pl.BlockSpec((None, d, rows), lambda i, e: (e, 0, 0)),
            pl.BlockSpec((PEER_HEADS, EXPERT_ROWS_I, tt), lambda i, e: (0, e, i)),
            pl.BlockSpec((PEER_HEADS, N_KEYS, tt), lambda i, e: (0, 0, i)),
            pl.BlockSpec((PEER_HEADS, 1, tt), lambda i, e: (0, 0, i)),
            pl.BlockSpec((tt, d), lambda i, e: (i, 0)),
            pl.BlockSpec(gf.shape, lambda i, e: (0, 0)),
        ],
        out_specs=pl.BlockSpec((tt, d), lambda i, e: (i, 0)),
        out_shape=jax.ShapeDtypeStruct((t, d), jnp.float32),
        scratch_shapes=[
            pltpu.VMEM((2, EXPERT_CHUNK_I * N_KEYS, tt), jnp.float32),
            pltpu.VMEM((rows, tt), jnp.bfloat16),
            pltpu.VMEM((d, tt), jnp.float32),
            pltpu.VMEM((PEER_HEADS, EXPERT_ROWS_I, tt // LANES, SUBLANES, LANES), jnp.float32),
            pltpu.VMEM((PEER_HEADS, tt // LANES, SUBLANES, LANES), jnp.float32),
        ],
        compiler_params=pltpu.CompilerParams(
            dimension_semantics=("arbitrary", "arbitrary"), vmem_limit_bytes=VMEM_LIMIT_BYTES),
        name="experts",
    )(xnt, u, vt, e0, e1, theta, h1, gf)


def kernel(x, meta_tokens, norm1_g, w_in, conv_w, conv_b, pool_w, pool_scale, w_out, norm2_g,
           peer_w_q, peer_keys, peer_u, peer_v, final_norm_g):
    b, seq, d = x.shape
    bf = jnp.bfloat16
    h1 = _mixer(x, meta_tokens, norm1_g.reshape(1, d), w_in.astype(bf), conv_w,
                conv_b.reshape(1, -1), pool_w.astype(bf), pool_scale.reshape(1, -1), w_out.astype(bf))
    h1 = h1.reshape(b * seq, d)
    keys = peer_keys.reshape(2 * PEER_HEADS, N_KEYS, PEER_KEY_HALF).astype(bf)
    xnt, e0, e1, theta = _routing(h1, norm2_g.reshape(1, d), peer_w_q.T.astype(bf), keys)
    vt = peer_v.astype(bf).reshape(-1, EXPERT_ROWS_I * N_KEYS, d).transpose(0, 2, 1)
    out = _experts(xnt, peer_u.astype(bf), vt, e0, e1, theta, h1, final_norm_g.reshape(1, d))
    return out.reshape(b, seq, d)
```

```python
import functools
import math

import jax
import jax.numpy as jnp
from jax import lax
from jax.experimental import pallas as pl
from jax.experimental.pallas import tpu as pltpu

N_META = 16
EPS = 1e-6
CONV_CH = 512
POOL_WINDOWS = (2, 4, 8, 16)
POOL_GROUP_DIM = 128
PEER_HEADS = 8
N_KEYS = 128
PEER_KEY_HALF = 128
PEER_TOPK = 16

LANES = 128
SUBLANES = 8
VMEM_LIMIT_BYTES = 56 * 1024 * 1024

HALO = 16
MIX_TILE = 1024
ROUTE_TILE = SUBLANES * LANES
GROUP_PITCH = N_KEYS + SUBLANES
EXPERT_TILE_TOKENS = 512
EXPERT_ROWS_I = 16
EXPERT_CHUNK_I = 2

_SQRT_HALF = math.sqrt(0.5)


def _rms_norm_f32(v, g):
    return v * lax.rsqrt(jnp.mean(v * v, axis=-1, keepdims=True) + EPS) * g


def _mixer_kernel(seq_len, xprev_ref, x_ref, xnext_ref, meta_ref, g1_ref, win_ref, convw_ref,
                  convb_ref, poolw_ref, pscale_ref, wout_ref, h1_ref, hn_scr, p_scr, z_scr, y_scr):
    ts = x_ref.shape[1]
    j = pl.program_id(1)
    last = pl.num_programs(1) - 1
    g1 = g1_ref[...]

    x = x_ref[0]
    prev = jnp.where(j == 0, meta_ref[...], xprev_ref[0])
    nxt = jnp.where(j == last, 0.0, _rms_norm_f32(xnext_ref[0], g1))
    hn_scr[0:HALO, :] = _rms_norm_f32(prev, g1).astype(jnp.bfloat16)
    hn_scr[HALO:HALO + ts, :] = _rms_norm_f32(x, g1).astype(jnp.bfloat16)
    hn_scr[HALO + ts:HALO + ts + HALO, :] = nxt.astype(jnp.bfloat16)

    p_scr[...] = jnp.dot(hn_scr[...], win_ref[...], preferred_element_type=jnp.float32)

    z_scr[...] = p_scr[:, 2 * CONV_CH:3 * CONV_CH] * p_scr[:, 0:CONV_CH]
    conv = (z_scr[HALO - 1:HALO - 1 + ts, :] * convw_ref[0:1, :]
            + z_scr[HALO:HALO + ts, :] * convw_ref[1:2, :]
            + z_scr[HALO + 1:HALO + 1 + ts, :] * convw_ref[2:3, :]
            + convb_ref[...])
    ya = p_scr[HALO:HALO + ts, CONV_CH:2 * CONV_CH] * conv
    y_scr[:, 0:CONV_CH] = ya.astype(jnp.bfloat16)

    pos = j * ts + lax.broadcasted_iota(jnp.int32, (ts, 1), 0)
    hb0 = 3 * CONV_CH
    for g, w in enumerate(POOL_WINDOWS):
        c0 = hb0 + g * POOL_GROUP_DIM
        c1 = c0 + POOL_GROUP_DIM
        acc = p_scr[HALO - w // 2:HALO - w // 2 + ts, c0:c1]
        for k in range(-w // 2 + 1, w // 2):
            acc = acc + p_scr[HALO + k:HALO + k + ts, c0:c1]
        cnt = (w - jnp.maximum(pos + w // 2 - seq_len, 0)).astype(jnp.float32)
        mixed = acc / cnt - p_scr[HALO:HALO + ts, c0:c1]
        yb = jnp.dot(mixed.astype(jnp.bfloat16), poolw_ref[g], preferred_element_type=jnp.float32)
        yb = yb * pscale_ref[:, g * POOL_GROUP_DIM:(g + 1) * POOL_GROUP_DIM]
        y_scr[:, CONV_CH + g * POOL_GROUP_DIM:CONV_CH + (g + 1) * POOL_GROUP_DIM] = yb.astype(jnp.bfloat16)

    h1_ref[0] = x + jnp.dot(y_scr[...], wout_ref[...], preferred_element_type=jnp.float32)


def _mixer(x, meta, g1, win, convw, convb, poolw, pscale, wout):
    b, seq, d = x.shape
    ts = MIX_TILE
    assert seq % ts == 0 and ts % HALO == 0 and meta.shape[0] == HALO
    nt = seq // ts
    hb = ts // HALO
    nhb = seq // HALO
    rows = ts + 2 * HALO
    full = lambda shape: pl.BlockSpec(shape, lambda bi, j: (0,) * len(shape),
                                      pipeline_mode=pl.Buffered(1))
    return pl.pallas_call(
        functools.partial(_mixer_kernel, seq),
        grid=(b, nt),
        in_specs=[
            pl.BlockSpec((1, HALO, d), lambda bi, j: (bi, jnp.maximum(j * hb - 1, 0), 0)),
            pl.BlockSpec((1, ts, d), lambda bi, j: (bi, j, 0)),
            pl.BlockSpec((1, HALO, d), lambda bi, j: (bi, jnp.minimum((j + 1) * hb, nhb - 1), 0)),
            full(meta.shape), full(g1.shape), full(win.shape), full(convw.shape), full(convb.shape),
            full(poolw.shape), full(pscale.shape), full(wout.shape),
        ],
        out_specs=pl.BlockSpec((1, ts, d), lambda bi, j: (bi, j, 0)),
        out_shape=jax.ShapeDtypeStruct((b, seq, d), jnp.float32),
        scratch_shapes=[
            pltpu.VMEM((rows, d), jnp.bfloat16),
            pltpu.VMEM((rows, win.shape[1]), jnp.float32),
            pltpu.VMEM((rows, CONV_CH), jnp.float32),
            pltpu.VMEM((ts, d), jnp.bfloat16),
        ],
        compiler_params=pltpu.CompilerParams(
            dimension_semantics=("arbitrary", "arbitrary"), vmem_limit_bytes=VMEM_LIMIT_BYTES),
        name="mixer",
    )(x, x, x, meta, g1, win, convw, convb, poolw, pscale, wout)


def _cmpx(v, a, b):
    hi = jnp.maximum(v[a], v[b])
    lo = jnp.minimum(v[a], v[b])
    v[a], v[b] = hi, lo


def _oddeven_merge_sort_pairs(n):
    pairs = []
    p = 1
    while p < n:
        k = p
        while k >= 1:
            for jj in range(k % p, n - k, 2 * k):
                for i in range(min(k, n - jj - k)):
                    if (i + jj) // (2 * p) == (i + jj + k) // (2 * p):
                        pairs.append((i + jj, i + jj + k))
            k //= 2
        p *= 2
    return pairs


_SORT16 = _oddeven_merge_sort_pairs(PEER_TOPK)


def _sort_desc(v):
    v = list(v)
    for a, b in _SORT16:
        _cmpx(v, a, b)
    return v


def _bitonic_sort_desc(v):
    v = list(v)
    d = len(v) // 2
    while d >= 1:
        for i in range(len(v)):
            if i & d == 0:
                _cmpx(v, i, i + d)
        d //= 2
    return v


def _top16_select(xs, ys):
    n = len(xs)
    out = list(xs)
    for m, y in enumerate(ys):
        out[n - 1 - m] = jnp.maximum(out[n - 1 - m], y)
    return out


def _routing_kernel(h1_ref, g2_ref, wqt_ref, keys_ref, xnt_ref, e0_ref, e1_ref, theta_ref,
                    xnb_scr, slab_scr):
    tr = h1_ref.shape[0]
    ngroups = tr // LANES
    xn = _rms_norm_f32(h1_ref[...], g2_ref[...])
    xnb_scr[...] = xn.astype(jnp.bfloat16)
    xnt_ref[...] = xn.T.astype(jnp.bfloat16)

    def slab_rows(g):
        return pl.ds(g * GROUP_PITCH, N_KEYS)

    def load_key(hp, n):
        return slab_scr[hp, pl.ds(n, ngroups, stride=GROUP_PITCH), :]

    def top16(hp):
        best = None
        for blk in range(N_KEYS // PEER_TOPK):
            cur = _sort_desc([load_key(hp, blk * PEER_TOPK + m) for m in range(PEER_TOPK)])
            best = cur if best is None else _bitonic_sort_desc(_top16_select(best, cur))
        return best

    def head_body(h, carry):
        q0 = pl.multiple_of(h * 2 * PEER_KEY_HALF, 2 * PEER_KEY_HALF)
        qt = lax.dot_general(wqt_ref[pl.ds(q0, 2 * PEER_KEY_HALF), :], xnb_scr[...],
                             (((1,), (1,)), ((), ())),
                             preferred_element_type=jnp.float32).astype(jnp.bfloat16)
        for half in range(2):
            s = jnp.dot(keys_ref[2 * h + half], qt[half * PEER_KEY_HALF:(half + 1) * PEER_KEY_HALF, :],
                        preferred_element_type=jnp.float32)
            for g in range(ngroups):
                slab_scr[half, slab_rows(g), :] = s[:, g * LANES:(g + 1) * LANES]
        a = top16(0)
        b = top16(1)
        cmax = a[0] + b[0]
        rows = [[a[i] + b[jj] for jj in range(PEER_TOPK // (i + 1))] for i in range(PEER_TOPK)]
        cands = [(i, jj, rows[i][jj]) for i in range(PEER_TOPK) for jj in range(len(rows[i]))]
        best = rows[0]
        singles = [rows[i][0] for i in range(PEER_TOPK // 2, PEER_TOPK)]
        merge_rows = [rows[i] for i in range(1, PEER_TOPK // 2)] + [singles]
        for r, row in enumerate(merge_rows):
            best = _top16_select(best, row)
            if r + 1 < len(merge_rows):
                best = _bitonic_sort_desc(best)
        tau = functools.reduce(jnp.minimum, best)
        z = functools.reduce(lambda u, v: u + v, [jnp.exp(c - cmax) for c in best])
        inv_z = 0.5 / z
        e0s = [jnp.exp(a[i] - a[0]) * inv_z for i in range(PEER_TOPK)]
        e1s = [jnp.exp(b[jj] - b[0]) for jj in range(PEER_TOPK)]
        theta = None
        for i, jj, c in cands:
            pij = jnp.where(c >= tau, e0s[i] * e1s[jj], jnp.inf)
            theta = pij if theta is None else jnp.minimum(theta, pij)
        for g in range(ngroups):
            lanes = slice(g * LANES, (g + 1) * LANES)
            a0 = a[0][g:g + 1, :]
            b0 = b[0][g:g + 1, :]
            iz = inv_z[g:g + 1, :]
            e0_ref[h, :, lanes] = jnp.exp(slab_scr[0, slab_rows(g), :] - a0) * iz
            e1_ref[h, :, lanes] = jnp.exp(slab_scr[1, slab_rows(g), :] - b0)
            theta_ref[h, :, lanes] = theta[g:g + 1, :]
        return carry

    lax.fori_loop(0, PEER_HEADS, head_body, 0)


def _routing(h1, g2, wqt, keys):
    t, d = h1.shape
    tr = ROUTE_TILE
    assert t % tr == 0
    ngroups = tr // LANES
    full = lambda shape: pl.BlockSpec(shape, lambda i: (0,) * len(shape),
                                      pipeline_mode=pl.Buffered(1))
    return pl.pallas_call(
        _routing_kernel,
        grid=(t // tr,),
        in_specs=[pl.BlockSpec((tr, d), lambda i: (i, 0)), full(g2.shape), full(wqt.shape),
                  full(keys.shape)],
        out_specs=[
            pl.BlockSpec((d, tr), lambda i: (0, i)),
            pl.BlockSpec((PEER_HEADS, N_KEYS, tr), lambda i: (0, 0, i)),
            pl.BlockSpec((PEER_HEADS, N_KEYS, tr), lambda i: (0, 0, i)),
            pl.BlockSpec((PEER_HEADS, 1, tr), lambda i: (0, 0, i)),
        ],
        out_shape=[
            jax.ShapeDtypeStruct((d, t), jnp.bfloat16),
            jax.ShapeDtypeStruct((PEER_HEADS, N_KEYS, t), jnp.float32),
            jax.ShapeDtypeStruct((PEER_HEADS, N_KEYS, t), jnp.float32),
            jax.ShapeDtypeStruct((PEER_HEADS, 1, t), jnp.float32),
        ],
        scratch_shapes=[
            pltpu.VMEM((tr, d), jnp.bfloat16),
            pltpu.VMEM((2, ngroups * GROUP_PITCH, LANES), jnp.float32),
        ],
        compiler_params=pltpu.CompilerParams(
            dimension_semantics=("arbitrary",), vmem_limit_bytes=VMEM_LIMIT_BYTES),
        name="routing",
    )(h1, g2, wqt, keys)


def _expert_kernel(xnt_ref, u_ref, vt_ref, e0_ref, e1_ref, theta_ref, h1_ref, gf_ref, out_ref,
                   pre_scr, y_scr, acc_scr):
    e = pl.program_id(1)
    tt = xnt_ref.shape[1]

    @pl.when(e == 0)
    def _():
        acc_scr[...] = jnp.zeros_like(acc_scr)

    n_chunks = EXPERT_ROWS_I // EXPERT_CHUNK_I
    chunk_rows = EXPERT_CHUNK_I * N_KEYS

    def pre_activations(c):
        pre_scr[c % 2] = jnp.dot(u_ref[c * chunk_rows:(c + 1) * chunk_rows, :], xnt_ref[...],
                                 preferred_element_type=jnp.float32)

    def output_part(c):
        c0 = c * chunk_rows
        acc_scr[...] += jnp.dot(vt_ref[:, c0:c0 + chunk_rows], y_scr[c0:c0 + chunk_rows, :],
                                preferred_element_type=jnp.float32)

    pre_activations(0)
    for c in range(n_chunks + 1):
        for ic in range(EXPERT_CHUNK_I):
            if ic == 1 % EXPERT_CHUNK_I and c + 1 < n_chunks:
                pre_activations(c + 1)
            if ic == 2 % EXPERT_CHUNK_I and c >= 1:
                output_part(c - 1)
            if c == n_chunks:
                continue
            il = c * EXPERT_CHUNK_I + ic
            for tg in range(tt // LANES):
                lanes = slice(tg * LANES, (tg + 1) * LANES)
                e0v = [jnp.broadcast_to(e0_ref[h, il:il + 1, lanes], (SUBLANES, LANES))
                       for h in range(PEER_HEADS)]
                thv = [jnp.broadcast_to(theta_ref[h, :, lanes], (SUBLANES, LANES))
                       for h in range(PEER_HEADS)]
                for jp in range(N_KEYS // (2 * SUBLANES)):
                    ys = []
                    for half in range(2):
                        j0 = (2 * jp + half) * SUBLANES
                        gate = None
                        for h in range(PEER_HEADS):
                            p = e0v[h] * e1_ref[h, j0:j0 + SUBLANES, lanes]
                            sel = jnp.where(p >= thv[h], p, 0.0)
                            gate = sel if gate is None else gate + sel
                        a = pre_scr[c % 2, ic * N_KEYS + j0:ic * N_KEYS + j0 + SUBLANES, lanes]
                        act = a * (1.0 + lax.erf(a * _SQRT_HALF))
                        ys.append(gate * act)
                    r0 = il * N_KEYS + 2 * jp * SUBLANES
                    y_scr[r0:r0 + 2 * SUBLANES, lanes] = (
                        jnp.concatenate(ys, axis=0).astype(jnp.bfloat16))

    @pl.when(e == pl.num_programs(1) - 1)
    def _():
        h2 = h1_ref[...] + acc_scr[...].T
        out_ref[...] = _rms_norm_f32(h2, gf_ref[...])


def _experts(xnt, u, vt, e0, e1, theta, h1, gf):
    d, t = xnt.shape
    tt = EXPERT_TILE_TOKENS
    rows = EXPERT_ROWS_I * N_KEYS
    n_exp = u.shape[0]
    assert t % tt == 0 and n_exp % rows == 0
    return pl.pallas_call(
        _expert_kernel,
        grid=(t // tt, n_exp // rows),
        in_specs=[
            pl.BlockSpec((d, tt), lambda i, e: (0, i)),
            pl.BlockSpec((rows, d), lambda i, e: (e, 0)),
            pl.BlockSpec((None, d, rows), lambda i, e: (e, 0, 0)),
            pl.BlockSpec((PEER_HEADS, EXPERT_ROWS_I, tt), lambda i, e: (0, e, i)),
            pl.BlockSpec((PEER_HEADS, N_KEYS, tt), lambda i, e: (0, 0, i)),
            pl.BlockSpec((PEER_HEADS, 1, tt), lambda i, e: (0, 0, i)),
            pl.BlockSpec((tt, d), lambda i, e: (i, 0)),
            pl.BlockSpec(gf.shape, lambda i, e: (0, 0)),
        ],
        out_specs=pl.BlockSpec((tt, d), lambda i, e: (i, 0)),
        out_shape=jax.ShapeDtypeStruct((t, d), jnp.float32),
        scratch_shapes=[
            pltpu.VMEM((2, EXPERT_CHUNK_I * N_KEYS, tt), jnp.float32),
            pltpu.VMEM((rows, tt), jnp.bfloat16),
            pltpu.VMEM((d, tt), jnp.float32),
        ],
        compiler_params=pltpu.CompilerParams(
            dimension_semantics=("arbitrary", "arbitrary"), vmem_limit_bytes=VMEM_LIMIT_BYTES),
        name="experts",
    )(xnt, u, vt, e0, e1, theta, h1, gf)


def kernel(x, meta_tokens, norm1_g, w_in, conv_w, conv_b, pool_w, pool_scale, w_out, norm2_g,
           peer_w_q, peer_keys, peer_u, peer_v, final_norm_g):
    b, seq, d = x.shape
    bf = jnp.bfloat16
    h1 = _mixer(x, meta_tokens, norm1_g.reshape(1, d), w_in.astype(bf), conv_w,
                conv_b.reshape(1, -1), pool_w.astype(bf), pool_scale.reshape(1, -1), w_out.astype(bf))
    h1 = h1.reshape(b * seq, d)
    keys = peer_keys.reshape(2 * PEER_HEADS, N_KEYS, PEER_KEY_HALF).astype(bf)
    xnt, e0, e1, theta = _routing(h1, norm2_g.reshape(1, d), peer_w_q.T.astype(bf), keys)
    vt = peer_v.astype(bf).reshape(-1, EXPERT_ROWS_I * N_KEYS, d).transpose(0, 2, 1)
    out = _experts(xnt, peer_u.astype(bf), vt, e0, e1, theta, h1, final_norm_g.reshape(1, d))
    return out.reshape(b, seq, d)
```

```python
import functools
import math

import jax
import jax.numpy as jnp
from jax import lax
from jax.experimental import pallas as pl
from jax.experimental.pallas import tpu as pltpu

N_META = 16
EPS = 1e-6
CONV_CH = 512
POOL_WINDOWS = (2, 4, 8, 16)
POOL_GROUP_DIM = 128
PEER_HEADS = 8
N_KEYS = 128
PEER_KEY_HALF = 128
PEER_TOPK = 16

LANES = 128
SUBLANES = 8
VMEM_LIMIT_BYTES = 56 * 1024 * 1024

HALO = 16
MIX_TILE = 1024
ROUTE_TILE = SUBLANES * LANES
GROUP_PITCH = N_KEYS + SUBLANES
EXPERT_TILE_TOKENS = 512
EXPERT_ROWS_I = 16
EXPERT_CHUNK_I = 8

_SQRT_HALF = math.sqrt(0.5)


def _rms_norm_f32(v, g):
    return v * lax.rsqrt(jnp.mean(v * v, axis=-1, keepdims=True) + EPS) * g


def _mixer_kernel(seq_len, xprev_ref, x_ref, xnext_ref, meta_ref, g1_ref, win_ref, convw_ref,
                  convb_ref, poolw_ref, pscale_ref, wout_ref, h1_ref, hn_scr, p_scr, z_scr, y_scr):
    ts = x_ref.shape[1]
    j = pl.program_id(1)
    last = pl.num_programs(1) - 1
    g1 = g1_ref[...]

    x = x_ref[0]
    prev = jnp.where(j == 0, meta_ref[...], xprev_ref[0])
    nxt = jnp.where(j == last, 0.0, _rms_norm_f32(xnext_ref[0], g1))
    hn_scr[0:HALO, :] = _rms_norm_f32(prev, g1).astype(jnp.bfloat16)
    hn_scr[HALO:HALO + ts, :] = _rms_norm_f32(x, g1).astype(jnp.bfloat16)
    hn_scr[HALO + ts:HALO + ts + HALO, :] = nxt.astype(jnp.bfloat16)

    p_scr[...] = jnp.dot(hn_scr[...], win_ref[...], preferred_element_type=jnp.float32)

    z_scr[...] = p_scr[:, 2 * CONV_CH:3 * CONV_CH] * p_scr[:, 0:CONV_CH]
    conv = (z_scr[HALO - 1:HALO - 1 + ts, :] * convw_ref[0:1, :]
            + z_scr[HALO:HALO + ts, :] * convw_ref[1:2, :]
            + z_scr[HALO + 1:HALO + 1 + ts, :] * convw_ref[2:3, :]
            + convb_ref[...])
    ya = p_scr[HALO:HALO + ts, CONV_CH:2 * CONV_CH] * conv
    y_scr[:, 0:CONV_CH] = ya.astype(jnp.bfloat16)

    pos = j * ts + lax.broadcasted_iota(jnp.int32, (ts, 1), 0)
    hb0 = 3 * CONV_CH
    for g, w in enumerate(POOL_WINDOWS):
        c0 = hb0 + g * POOL_GROUP_DIM
        c1 = c0 + POOL_GROUP_DIM
        acc = p_scr[HALO - w // 2:HALO - w // 2 + ts, c0:c1]
        for k in range(-w // 2 + 1, w // 2):
            acc = acc + p_scr[HALO + k:HALO + k + ts, c0:c1]
        cnt = (w - jnp.maximum(pos + w // 2 - seq_len, 0)).astype(jnp.float32)
        mixed = acc / cnt - p_scr[HALO:HALO + ts, c0:c1]
        yb = jnp.dot(mixed.astype(jnp.bfloat16), poolw_ref[g], preferred_element_type=jnp.float32)
        yb = yb * pscale_ref[:, g * POOL_GROUP_DIM:(g + 1) * POOL_GROUP_DIM]
        y_scr[:, CONV_CH + g * POOL_GROUP_DIM:CONV_CH + (g + 1) * POOL_GROUP_DIM] = yb.astype(jnp.bfloat16)

    h1_ref[0] = x + jnp.dot(y_scr[...], wout_ref[...], preferred_element_type=jnp.float32)


def _mixer(x, meta, g1, win, convw, convb, poolw, pscale, wout):
    b, seq, d = x.shape
    ts = MIX_TILE
    assert seq % ts == 0 and ts % HALO == 0 and meta.shape[0] == HALO
    nt = seq // ts
    hb = ts // HALO
    nhb = seq // HALO
    rows = ts + 2 * HALO
    full = lambda shape: pl.BlockSpec(shape, lambda bi, j: (0,) * len(shape),
                                      pipeline_mode=pl.Buffered(1))
    return pl.pallas_call(
        functools.partial(_mixer_kernel, seq),
        grid=(b, nt),
        in_specs=[
            pl.BlockSpec((1, HALO, d), lambda bi, j: (bi, jnp.maximum(j * hb - 1, 0), 0)),
            pl.BlockSpec((1, ts, d), lambda bi, j: (bi, j, 0)),
            pl.BlockSpec((1, HALO, d), lambda bi, j: (bi, jnp.minimum((j + 1) * hb, nhb - 1), 0)),
            full(meta.shape), full(g1.shape), full(win.shape), full(convw.shape), full(convb.shape),
            full(poolw.shape), full(pscale.shape), full(wout.shape),
        ],
        out_specs=pl.BlockSpec((1, ts, d), lambda bi, j: (bi, j, 0)),
        out_shape=jax.ShapeDtypeStruct((b, seq, d), jnp.float32),
        scratch_shapes=[
            pltpu.VMEM((rows, d), jnp.bfloat16),
            pltpu.VMEM((rows, win.shape[1]), jnp.float32),
            pltpu.VMEM((rows, CONV_CH), jnp.float32),
            pltpu.VMEM((ts, d), jnp.bfloat16),
        ],
        compiler_params=pltpu.CompilerParams(
            dimension_semantics=("arbitrary", "arbitrary"), vmem_limit_bytes=VMEM_LIMIT_BYTES),
        name="mixer",
    )(x, x, x, meta, g1, win, convw, convb, poolw, pscale, wout)


def _cmpx(v, a, b):
    hi = jnp.maximum(v[a], v[b])
    lo = jnp.minimum(v[a], v[b])
    v[a], v[b] = hi, lo


def _oddeven_merge_sort_pairs(n):
    pairs = []
    p = 1
    while p < n:
        k = p
        while k >= 1:
            for jj in range(k % p, n - k, 2 * k):
                for i in range(min(k, n - jj - k)):
                    if (i + jj) // (2 * p) == (i + jj + k) // (2 * p):
                        pairs.append((i + jj, i + jj + k))
            k //= 2
        p *= 2
    return pairs


_SORT16 = _oddeven_merge_sort_pairs(PEER_TOPK)


def _sort_desc(v):
    v = list(v)
    for a, b in _SORT16:
        _cmpx(v, a, b)
    return v


def _bitonic_sort_desc(v):
    v = list(v)
    d = len(v) // 2
    while d >= 1:
        for i in range(len(v)):
            if i & d == 0:
                _cmpx(v, i, i + d)
        d //= 2
    return v


def _top16_select(xs, ys):
    n = len(xs)
    out = list(xs)
    for m, y in enumerate(ys):
        out[n - 1 - m] = jnp.maximum(out[n - 1 - m], y)
    return out


def _routing_kernel(h1_ref, g2_ref, wqt_ref, keys_ref, xnt_ref, e0_ref, e1_ref, theta_ref,
                    xnb_scr, slab_scr):
    tr = h1_ref.shape[0]
    ngroups = tr // LANES
    xn = _rms_norm_f32(h1_ref[...], g2_ref[...])
    xnb_scr[...] = xn.astype(jnp.bfloat16)
    xnt_ref[...] = xn.T.astype(jnp.bfloat16)

    def slab_rows(g):
        return pl.ds(g * GROUP_PITCH, N_KEYS)

    def load_key(hp, n):
        return slab_scr[hp, pl.ds(n, ngroups, stride=GROUP_PITCH), :]

    def top16(hp):
        best = None
        for blk in range(N_KEYS // PEER_TOPK):
            cur = _sort_desc([load_key(hp, blk * PEER_TOPK + m) for m in range(PEER_TOPK)])
            best = cur if best is None else _bitonic_sort_desc(_top16_select(best, cur))
        return best

    def head_body(h, carry):
        q0 = pl.multiple_of(h * 2 * PEER_KEY_HALF, 2 * PEER_KEY_HALF)
        qt = lax.dot_general(wqt_ref[pl.ds(q0, 2 * PEER_KEY_HALF), :], xnb_scr[...],
                             (((1,), (1,)), ((), ())),
                             preferred_element_type=jnp.float32).astype(jnp.bfloat16)
        for half in range(2):
            s = jnp.dot(keys_ref[2 * h + half], qt[half * PEER_KEY_HALF:(half + 1) * PEER_KEY_HALF, :],
                        preferred_element_type=jnp.float32)
            for g in range(ngroups):
                slab_scr[half, slab_rows(g), :] = s[:, g * LANES:(g + 1) * LANES]
        a = top16(0)
        b = top16(1)
        cmax = a[0] + b[0]
        rows = [[a[i] + b[jj] for jj in range(PEER_TOPK // (i + 1))] for i in range(PEER_TOPK)]
        cands = [(i, jj, rows[i][jj]) for i in range(PEER_TOPK) for jj in range(len(rows[i]))]
        best = rows[0]
        singles = [rows[i][0] for i in range(PEER_TOPK // 2, PEER_TOPK)]
        merge_rows = [rows[i] for i in range(1, PEER_TOPK // 2)] + [singles]
        for r, row in enumerate(merge_rows):
            best = _top16_select(best, row)
            if r + 1 < len(merge_rows):
                best = _bitonic_sort_desc(best)
        tau = functools.reduce(jnp.minimum, best)
        z = functools.reduce(lambda u, v: u + v, [jnp.exp(c - cmax) for c in best])
        inv_z = 0.5 / z
        e0s = [jnp.exp(a[i] - a[0]) * inv_z for i in range(PEER_TOPK)]
        e1s = [jnp.exp(b[jj] - b[0]) for jj in range(PEER_TOPK)]
        theta = None
        for i, jj, c in cands:
            pij = jnp.where(c >= tau, e0s[i] * e1s[jj], jnp.inf)
            theta = pij if theta is None else jnp.minimum(theta, pij)
        for g in range(ngroups):
            lanes = slice(g * LANES, (g + 1) * LANES)
            a0 = a[0][g:g + 1, :]
            b0 = b[0][g:g + 1, :]
            iz = inv_z[g:g + 1, :]
            e0_ref[h, :, lanes] = jnp.exp(slab_scr[0, slab_rows(g), :] - a0) * iz
            e1_ref[h, :, lanes] = jnp.exp(slab_scr[1, slab_rows(g), :] - b0)
            theta_ref[h, :, lanes] = theta[g:g + 1, :]
        return carry

    lax.fori_loop(0, PEER_HEADS, head_body, 0)


def _routing(h1, g2, wqt, keys):
    t, d = h1.shape
    tr = ROUTE_TILE
    assert t % tr == 0
    ngroups = tr // LANES
    full = lambda shape: pl.BlockSpec(shape, lambda i: (0,) * len(shape),
                                      pipeline_mode=pl.Buffered(1))
    return pl.pallas_call(
        _routing_kernel,
        grid=(t // tr,),
        in_specs=[pl.BlockSpec((tr, d), lambda i: (i, 0)), full(g2.shape), full(wqt.shape),
                  full(keys.shape)],
        out_specs=[
            pl.BlockSpec((d, tr), lambda i: (0, i)),
            pl.BlockSpec((PEER_HEADS, N_KEYS, tr), lambda i: (0, 0, i)),
            pl.BlockSpec((PEER_HEADS, N_KEYS, tr), lambda i: (0, 0, i)),
            pl.BlockSpec((PEER_HEADS, 1, tr), lambda i: (0, 0, i)),
        ],
        out_shape=[
            jax.ShapeDtypeStruct((d, t), jnp.bfloat16),
            jax.ShapeDtypeStruct((PEER_HEADS, N_KEYS, t), jnp.float32),
            jax.ShapeDtypeStruct((PEER_HEADS, N_KEYS, t), jnp.float32),
            jax.ShapeDtypeStruct((PEER_HEADS, 1, t), jnp.float32),
        ],
        scratch_shapes=[
            pltpu.VMEM((tr, d), jnp.bfloat16),
            pltpu.VMEM((2, ngroups * GROUP_PITCH, LANES), jnp.float32),
        ],
        compiler_params=pltpu.CompilerParams(
            dimension_semantics=("arbitrary",), vmem_limit_bytes=VMEM_LIMIT_BYTES),
        name="routing",
    )(h1, g2, wqt, keys)


def _expert_kernel(xnt_ref, u_ref, vt_ref, e0_ref, e1_ref, theta_ref, h1_ref, gf_ref, out_ref,
                   pre_scr, y_scr, acc_scr):
    e = pl.program_id(1)
    tt = xnt_ref.shape[1]

    @pl.when(e == 0)
    def _():
        acc_scr[...] = jnp.zeros_like(acc_scr)

    n_chunks = EXPERT_ROWS_I // EXPERT_CHUNK_I
    chunk_rows = EXPERT_CHUNK_I * N_KEYS

    def pre_activations(c):
        pre_scr[c % 2] = jnp.dot(u_ref[c * chunk_rows:(c + 1) * chunk_rows, :], xnt_ref[...],
                                 preferred_element_type=jnp.float32)

    def output_part(c):
        c0 = c * chunk_rows
        acc_scr[...] += jnp.dot(vt_ref[:, c0:c0 + chunk_rows], y_scr[c0:c0 + chunk_rows, :],
                                preferred_element_type=jnp.float32)

    pre_activations(0)
    for c in range(n_chunks + 1):
        for ic in range(EXPERT_CHUNK_I):
            if ic == 1 and c + 1 < n_chunks:
                pre_activations(c + 1)
            if ic == 2 and c >= 1:
                output_part(c - 1)
            if c == n_chunks:
                continue
            il = c * EXPERT_CHUNK_I + ic
            for tg in range(tt // LANES):
                lanes = slice(tg * LANES, (tg + 1) * LANES)
                e0v = [jnp.broadcast_to(e0_ref[h, il:il + 1, lanes], (SUBLANES, LANES))
                       for h in range(PEER_HEADS)]
                thv = [jnp.broadcast_to(theta_ref[h, :, lanes], (SUBLANES, LANES))
                       for h in range(PEER_HEADS)]
                for jp in range(N_KEYS // (2 * SUBLANES)):
                    ys = []
                    for half in range(2):
                        j0 = (2 * jp + half) * SUBLANES
                        gate = None
                        for h in range(PEER_HEADS):
                            p = e0v[h] * e1_ref[h, j0:j0 + SUBLANES, lanes]
                            sel = jnp.where(p >= thv[h], p, 0.0)
                            gate = sel if gate is None else gate + sel
                        a = pre_scr[c % 2, ic * N_KEYS + j0:ic * N_KEYS + j0 + SUBLANES, lanes]
                        act = a * (1.0 + lax.erf(a * _SQRT_HALF))
                        ys.append(gate * act)
                    r0 = il * N_KEYS + 2 * jp * SUBLANES
                    y_scr[r0:r0 + 2 * SUBLANES, lanes] = (
                        jnp.concatenate(ys, axis=0).astype(jnp.bfloat16))

    @pl.when(e == pl.num_programs(1) - 1)
    def _():
        h2 = h1_ref[...] + acc_scr[...].T
        out_ref[...] = _rms_norm_f32(h2, gf_ref[...])


def _experts(xnt, u, vt, e0, e1, theta, h1, gf):
    d, t = xnt.shape
    tt = EXPERT_TILE_TOKENS
    rows = EXPERT_ROWS_I * N_KEYS
    n_exp = u.shape[0]
    assert t % tt == 0 and n_exp % rows == 0
    return pl.pallas_call(
        _expert_kernel,
        grid=(t // tt, n_exp // rows),
        in_specs=[
            pl.BlockSpec((d, tt), lambda i, e: (0, i)),
            pl.BlockSpec((rows, d), lambda i, e: (e, 0)),
            pl.BlockSpec((None, d, rows), lambda i, e: (e, 0, 0)),
            pl.BlockSpec((PEER_HEADS, EXPERT_ROWS_I, tt), lambda i, e: (0, e, i)),
            pl.BlockSpec((PEER_HEADS, N_KEYS, tt), lambda i, e: (0, 0, i)),
            pl.BlockSpec((PEER_HEADS, 1, tt), lambda i, e: (0, 0, i)),
            pl.BlockSpec((tt, d), lambda i, e: (i, 0)),
            pl.BlockSpec(gf.shape, lambda i, e: (0, 0)),
        ],
        out_specs=pl.BlockSpec((tt, d), lambda i, e: (i, 0)),
        out_shape=jax.ShapeDtypeStruct((t, d), jnp.float32),
        scratch_shapes=[
            pltpu.VMEM((2, EXPERT_CHUNK_I * N_KEYS, tt), jnp.float32),
            pltpu.VMEM((rows, tt), jnp.bfloat16),
            pltpu.VMEM((d, tt), jnp.float32),
        ],
        compiler_params=pltpu.CompilerParams(
            dimension_semantics=("arbitrary", "arbitrary"), vmem_limit_bytes=VMEM_LIMIT_BYTES),
        name="experts",
    )(xnt, u, vt, e0, e1, theta, h1, gf)


def kernel(x, meta_tokens, norm1_g, w_in, conv_w, conv_b, pool_w, pool_scale, w_out, norm2_g,
           peer_w_q, peer_keys, peer_u, peer_v, final_norm_g):
    b, seq, d = x.shape
    bf = jnp.bfloat16
    h1 = _mixer(x, meta_tokens, norm1_g.reshape(1, d), w_in.astype(bf), conv_w,
                conv_b.reshape(1, -1), pool_w.astype(bf), pool_scale.reshape(1, -1), w_out.astype(bf))
    h1 = h1.reshape(b * seq, d)
    keys = peer_keys.reshape(2 * PEER_HEADS, N_KEYS, PEER_KEY_HALF).astype(bf)
    xnt, e0, e1, theta = _routing(h1, norm2_g.reshape(1, d), peer_w_q.T.astype(bf), keys)
    vt = peer_v.astype(bf).reshape(-1, EXPERT_ROWS_I * N_KEYS, d).transpose(0, 2, 1)
    out = _experts(xnt, peer_u.astype(bf), vt, e0, e1, theta, h1, final_norm_g.reshape(1, d))
    return out.reshape(b, seq, d)
```

```python
import functools
import math

import jax
import jax.numpy as jnp
from jax import lax
from jax.experimental import pallas as pl
from jax.experimental.pallas import tpu as pltpu

N_META = 16
EPS = 1e-6
CONV_CH = 512
POOL_WINDOWS = (2, 4, 8, 16)
POOL_GROUP_DIM = 128
PEER_HEADS = 8
N_KEYS = 128
PEER_KEY_HALF = 128
PEER_TOPK = 16

LANES = 128
SUBLANES = 8
VMEM_LIMIT_BYTES = 56 * 1024 * 1024

HALO = 16
MIX_TILE = 1024
ROUTE_TILE = SUBLANES * LANES
GROUP_PITCH = N_KEYS + SUBLANES
EXPERT_TILE_TOKENS = 512
EXPERT_ROWS_I = 16
EXPERT_CHUNK_I = 4
GATE_BLOCK = 4

_SQRT_HALF = math.sqrt(0.5)


def _rms_norm_f32(v, g):
    return v * lax.rsqrt(jnp.mean(v * v, axis=-1, keepdims=True) + EPS) * g


def _mixer_kernel(seq_len, xprev_ref, x_ref, xnext_ref, meta_ref, g1_ref, win_ref, convw_ref,
                  convb_ref, poolw_ref, pscale_ref, wout_ref, h1_ref, hn_scr, p_scr, z_scr, y_scr):
    ts = x_ref.shape[1]
    j = pl.program_id(1)
    last = pl.num_programs(1) - 1
    g1 = g1_ref[...]

    x = x_ref[0]
    prev = jnp.where(j == 0, meta_ref[...], xprev_ref[0])
    nxt = jnp.where(j == last, 0.0, _rms_norm_f32(xnext_ref[0], g1))
    hn_scr[0:HALO, :] = _rms_norm_f32(prev, g1).astype(jnp.bfloat16)
    hn_scr[HALO:HALO + ts, :] = _rms_norm_f32(x, g1).astype(jnp.bfloat16)
    hn_scr[HALO + ts:HALO + ts + HALO, :] = nxt.astype(jnp.bfloat16)

    p_scr[...] = jnp.dot(hn_scr[...], win_ref[...], preferred_element_type=jnp.float32)

    z_scr[...] = p_scr[:, 2 * CONV_CH:3 * CONV_CH] * p_scr[:, 0:CONV_CH]
    conv = (z_scr[HALO - 1:HALO - 1 + ts, :] * convw_ref[0:1, :]
            + z_scr[HALO:HALO + ts, :] * convw_ref[1:2, :]
            + z_scr[HALO + 1:HALO + 1 + ts, :] * convw_ref[2:3, :]
            + convb_ref[...])
    ya = p_scr[HALO:HALO + ts, CONV_CH:2 * CONV_CH] * conv
    y_scr[:, 0:CONV_CH] = ya.astype(jnp.bfloat16)

    pos = j * ts + lax.broadcasted_iota(jnp.int32, (ts, 1), 0)
    hb0 = 3 * CONV_CH
    for g, w in enumerate(POOL_WINDOWS):
        c0 = hb0 + g * POOL_GROUP_DIM
        c1 = c0 + POOL_GROUP_DIM
        acc = p_scr[HALO - w // 2:HALO - w // 2 + ts, c0:c1]
        for k in range(-w // 2 + 1, w // 2):
            acc = acc + p_scr[HALO + k:HALO + k + ts, c0:c1]
        cnt = (w - jnp.maximum(pos + w // 2 - seq_len, 0)).astype(jnp.float32)
        mixed = acc / cnt - p_scr[HALO:HALO + ts, c0:c1]
        yb = jnp.dot(mixed.astype(jnp.bfloat16), poolw_ref[g], preferred_element_type=jnp.float32)
        yb = yb * pscale_ref[:, g * POOL_GROUP_DIM:(g + 1) * POOL_GROUP_DIM]
        y_scr[:, CONV_CH + g * POOL_GROUP_DIM:CONV_CH + (g + 1) * POOL_GROUP_DIM] = yb.astype(jnp.bfloat16)

    h1_ref[0] = x + jnp.dot(y_scr[...], wout_ref[...], preferred_element_type=jnp.float32)


def _mixer(x, meta, g1, win, convw, convb, poolw, pscale, wout):
    b, seq, d = x.shape
    ts = MIX_TILE
    assert seq % ts == 0 and ts % HALO == 0 and meta.shape[0] == HALO
    nt = seq // ts
    hb = ts // HALO
    nhb = seq // HALO
    rows = ts + 2 * HALO
    full = lambda shape: pl.BlockSpec(shape, lambda bi, j: (0,) * len(shape),
                                      pipeline_mode=pl.Buffered(1))
    return pl.pallas_call(
        functools.partial(_mixer_kernel, seq),
        grid=(b, nt),
        in_specs=[
            pl.BlockSpec((1, HALO, d), lambda bi, j: (bi, jnp.maximum(j * hb - 1, 0), 0)),
            pl.BlockSpec((1, ts, d), lambda bi, j: (bi, j, 0)),
            pl.BlockSpec((1, HALO, d), lambda bi, j: (bi, jnp.minimum((j + 1) * hb, nhb - 1), 0)),
            full(meta.shape), full(g1.shape), full(win.shape), full(convw.shape), full(convb.shape),
            full(poolw.shape), full(pscale.shape), full(wout.shape),
        ],
        out_specs=pl.BlockSpec((1, ts, d), lambda bi, j: (bi, j, 0)),
        out_shape=jax.ShapeDtypeStruct((b, seq, d), jnp.float32),
        scratch_shapes=[
            pltpu.VMEM((rows, d), jnp.bfloat16),
            pltpu.VMEM((rows, win.shape[1]), jnp.float32),
            pltpu.VMEM((rows, CONV_CH), jnp.float32),
            pltpu.VMEM((ts, d), jnp.bfloat16),
        ],
        compiler_params=pltpu.CompilerParams(
            dimension_semantics=("arbitrary", "arbitrary"), vmem_limit_bytes=VMEM_LIMIT_BYTES),
        name="mixer",
    )(x, x, x, meta, g1, win, convw, convb, poolw, pscale, wout)


def _cmpx(v, a, b):
    hi = jnp.maximum(v[a], v[b])
    lo = jnp.minimum(v[a], v[b])
    v[a], v[b] = hi, lo


def _oddeven_merge_sort_pairs(n):
    pairs = []
    p = 1
    while p < n:
        k = p
        while k >= 1:
            for jj in range(k % p, n - k, 2 * k):
                for i in range(min(k, n - jj - k)):
                    if (i + jj) // (2 * p) == (i + jj + k) // (2 * p):
                        pairs.append((i + jj, i + jj + k))
            k //= 2
        p *= 2
    return pairs


_SORT16 = _oddeven_merge_sort_pairs(PEER_TOPK)


def _sort_desc(v):
    v = list(v)
    for a, b in _SORT16:
        _cmpx(v, a, b)
    return v


def _bitonic_sort_desc(v):
    v = list(v)
    d = len(v) // 2
    while d >= 1:
        for i in range(len(v)):
            if i & d == 0:
                _cmpx(v, i, i + d)
        d //= 2
    return v


def _top16_select(xs, ys):
    n = len(xs)
    out = list(xs)
    for m, y in enumerate(ys):
        out[n - 1 - m] = jnp.maximum(out[n - 1 - m], y)
    return out


def _routing_kernel(h1_ref, g2_ref, wqt_ref, keys_ref, xnt_ref, e0_ref, e1_ref, theta_ref,
                    xnb_scr, slab_scr):
    tr = h1_ref.shape[0]
    ngroups = tr // LANES
    xn = _rms_norm_f32(h1_ref[...], g2_ref[...])
    xnb_scr[...] = xn.astype(jnp.bfloat16)
    xnt_ref[...] = xn.T.astype(jnp.bfloat16)

    def slab_rows(g):
        return pl.ds(g * GROUP_PITCH, N_KEYS)

    def load_key(hp, n):
        return slab_scr[hp, pl.ds(n, ngroups, stride=GROUP_PITCH), :]

    def top16(hp):
        best = None
        for blk in range(N_KEYS // PEER_TOPK):
            cur = _sort_desc([load_key(hp, blk * PEER_TOPK + m) for m in range(PEER_TOPK)])
            best = cur if best is None else _bitonic_sort_desc(_top16_select(best, cur))
        return best

    def head_body(h, carry):
        q0 = pl.multiple_of(h * 2 * PEER_KEY_HALF, 2 * PEER_KEY_HALF)
        qt = lax.dot_general(wqt_ref[pl.ds(q0, 2 * PEER_KEY_HALF), :], xnb_scr[...],
                             (((1,), (1,)), ((), ())),
                             preferred_element_type=jnp.float32).astype(jnp.bfloat16)
        for half in range(2):
            s = jnp.dot(keys_ref[2 * h + half], qt[half * PEER_KEY_HALF:(half + 1) * PEER_KEY_HALF, :],
                        preferred_element_type=jnp.float32)
            for g in range(ngroups):
                slab_scr[half, slab_rows(g), :] = s[:, g * LANES:(g + 1) * LANES]
        a = top16(0)
        b = top16(1)
        cmax = a[0] + b[0]
        rows = [[a[i] + b[jj] for jj in range(PEER_TOPK // (i + 1))] for i in range(PEER_TOPK)]
        cands = [(i, jj, rows[i][jj]) for i in range(PEER_TOPK) for jj in range(len(rows[i]))]
        best = rows[0]
        singles = [rows[i][0] for i in range(PEER_TOPK // 2, PEER_TOPK)]
        merge_rows = [rows[i] for i in range(1, PEER_TOPK // 2)] + [singles]
        for r, row in enumerate(merge_rows):
            best = _top16_select(best, row)
            if r + 1 < len(merge_rows):
                best = _bitonic_sort_desc(best)
        tau = functools.reduce(jnp.minimum, best)
        z = functools.reduce(lambda u, v: u + v, [jnp.exp(c - cmax) for c in best])
        inv_z = 0.5 / z
        e0s = [jnp.exp(a[i] - a[0]) * inv_z for i in range(PEER_TOPK)]
        e1s = [jnp.exp(b[jj] - b[0]) for jj in range(PEER_TOPK)]
        theta = None
        for i, jj, c in cands:
            pij = jnp.where(c >= tau, e0s[i] * e1s[jj], jnp.inf)
            theta = pij if theta is None else jnp.minimum(theta, pij)
        for g in range(ngroups):
            lanes = slice(g * LANES, (g + 1) * LANES)
            a0 = a[0][g:g + 1, :]
            b0 = b[0][g:g + 1, :]
            iz = inv_z[g:g + 1, :]
            e0_ref[h, :, lanes] = jnp.exp(slab_scr[0, slab_rows(g), :] - a0) * iz
            e1_ref[h, :, lanes] = jnp.exp(slab_scr[1, slab_rows(g), :] - b0)
            theta_ref[h, :, lanes] = theta[g:g + 1, :]
        return carry

    lax.fori_loop(0, PEER_HEADS, head_body, 0)


def _routing(h1, g2, wqt, keys):
    t, d = h1.shape
    tr = ROUTE_TILE
    assert t % tr == 0
    ngroups = tr // LANES
    full = lambda shape: pl.BlockSpec(shape, lambda i: (0,) * len(shape),
                                      pipeline_mode=pl.Buffered(1))
    return pl.pallas_call(
        _routing_kernel,
        grid=(t // tr,),
        in_specs=[pl.BlockSpec((tr, d), lambda i: (i, 0)), full(g2.shape), full(wqt.shape),
                  full(keys.shape)],
        out_specs=[
            pl.BlockSpec((d, tr), lambda i: (0, i)),
            pl.BlockSpec((PEER_HEADS, N_KEYS, tr), lambda i: (0, 0, i)),
            pl.BlockSpec((PEER_HEADS, N_KEYS, tr), lambda i: (0, 0, i)),
            pl.BlockSpec((PEER_HEADS, 1, tr), lambda i: (0, 0, i)),
        ],
        out_shape=[
            jax.ShapeDtypeStruct((d, t), jnp.bfloat16),
            jax.ShapeDtypeStruct((PEER_HEADS, N_KEYS, t), jnp.float32),
            jax.ShapeDtypeStruct((PEER_HEADS, N_KEYS, t), jnp.float32),
            jax.ShapeDtypeStruct((PEER_HEADS, 1, t), jnp.float32),
        ],
        scratch_shapes=[
            pltpu.VMEM((tr, d), jnp.bfloat16),
            pltpu.VMEM((2, ngroups * GROUP_PITCH, LANES), jnp.float32),
        ],
        compiler_params=pltpu.CompilerParams(
            dimension_semantics=("arbitrary",), vmem_limit_bytes=VMEM_LIMIT_BYTES),
        name="routing",
    )(h1, g2, wqt, keys)


def _expert_kernel(xnt_ref, u_ref, vt_ref, e0_ref, e1_ref, theta_ref, h1_ref, gf_ref, out_ref,
                   pre_scr, y_scr, acc_scr):
    e = pl.program_id(1)
    tt = xnt_ref.shape[1]

    @pl.when(e == 0)
    def _():
        acc_scr[...] = jnp.zeros_like(acc_scr)

    n_chunks = EXPERT_ROWS_I // EXPERT_CHUNK_I
    chunk_rows = EXPERT_CHUNK_I * N_KEYS

    def pre_activations(c):
        pre_scr[c % 2] = jnp.dot(u_ref[c * chunk_rows:(c + 1) * chunk_rows, :], xnt_ref[...],
                                 preferred_element_type=jnp.float32)

    def output_part(c):
        c0 = c * chunk_rows
        acc_scr[...] += jnp.dot(vt_ref[:, c0:c0 + chunk_rows], y_scr[c0:c0 + chunk_rows, :],
                                preferred_element_type=jnp.float32)

    pre_activations(0)
    for c in range(n_chunks + 1):
        for ic in range(EXPERT_CHUNK_I):
            if ic == 1 and c + 1 < n_chunks:
                pre_activations(c + 1)
            if ic == 2 and c >= 1:
                output_part(c - 1)
            if c == n_chunks:
                continue
            il = c * EXPERT_CHUNK_I + ic
            for tg in range(tt // LANES):
                lanes = slice(tg * LANES, (tg + 1) * LANES)
                for jb in range(N_KEYS // (GATE_BLOCK * SUBLANES)):
                    j0s = [(jb * GATE_BLOCK + k) * SUBLANES for k in range(GATE_BLOCK)]
                    gates = [None] * GATE_BLOCK
                    for h in range(PEER_HEADS):
                        e0v = jnp.broadcast_to(e0_ref[h, il:il + 1, lanes], (SUBLANES, LANES))
                        thv = jnp.broadcast_to(theta_ref[h, :, lanes], (SUBLANES, LANES))
                        for k, j0 in enumerate(j0s):
                            p = e0v * e1_ref[h, j0:j0 + SUBLANES, lanes]
                            sel = jnp.where(p >= thv, p, 0.0)
                            gates[k] = sel if gates[k] is None else gates[k] + sel
                    ys = []
                    for k, j0 in enumerate(j0s):
                        a = pre_scr[c % 2, ic * N_KEYS + j0:ic * N_KEYS + j0 + SUBLANES, lanes]
                        act = a * (1.0 + lax.erf(a * _SQRT_HALF))
                        ys.append(gates[k] * act)
                    r0 = il * N_KEYS + j0s[0]
                    y_scr[r0:r0 + GATE_BLOCK * SUBLANES, lanes] = (
                        jnp.concatenate(ys, axis=0).astype(jnp.bfloat16))

    @pl.when(e == pl.num_programs(1) - 1)
    def _():
        h2 = h1_ref[...] + acc_scr[...].T
        out_ref[...] = _rms_norm_f32(h2, gf_ref[...])


def _experts(xnt, u, vt, e0, e1, theta, h1, gf):
    d, t = xnt.shape
    tt = EXPERT_TILE_TOKENS
    rows = EXPERT_ROWS_I * N_KEYS
    n_exp = u.shape[0]
    assert t % tt == 0 and n_exp % rows == 0
    return pl.pallas_call(
        _expert_kernel,
        grid=(t // tt, n_exp // rows),
        in_specs=[
            pl.BlockSpec((d, tt), lambda i, e: (0, i)),
            pl.BlockSpec((rows, d), lambda i, e: (e, 0)),
            pl.BlockSpec((None, d, rows), lambda i, e: (e, 0, 0)),
            pl.BlockSpec((PEER_HEADS, EXPERT_ROWS_I, tt), lambda i, e: (0, e, i)),
            pl.BlockSpec((PEER_HEADS, N_KEYS, tt), lambda i, e: (0, 0, i)),
            pl.BlockSpec((PEER_HEADS, 1, tt), lambda i, e: (0, 0, i)),
            pl.BlockSpec((tt, d), lambda i, e: (i, 0)),
            pl.BlockSpec(gf.shape, lambda i, e: (0, 0)),
        ],
        out_specs=pl.BlockSpec((tt, d), lambda i, e: (i, 0)),
        out_shape=jax.ShapeDtypeStruct((t, d), jnp.float32),
        scratch_shapes=[
            pltpu.VMEM((2, EXPERT_CHUNK_I * N_KEYS, tt), jnp.float32),
            pltpu.VMEM((rows, tt), jnp.bfloat16),
            pltpu.VMEM((d, tt), jnp.float32),
        ],
        compiler_params=pltpu.CompilerParams(
            dimension_semantics=("arbitrary", "arbitrary"), vmem_limit_bytes=VMEM_LIMIT_BYTES),
        name="experts",
    )(xnt, u, vt, e0, e1, theta, h1, gf)


def kernel(x, meta_tokens, norm1_g, w_in, conv_w, conv_b, pool_w, pool_scale, w_out, norm2_g,
           peer_w_q, peer_keys, peer_u, peer_v, final_norm_g):
    b, seq, d = x.shape
    bf = jnp.bfloat16
    h1 = _mixer(x, meta_tokens, norm1_g.reshape(1, d), w_in.astype(bf), conv_w,
                conv_b.reshape(1, -1), pool_w.astype(bf), pool_scale.reshape(1, -1), w_out.astype(bf))
    h1 = h1.reshape(b * seq, d)
    keys = peer_keys.reshape(2 * PEER_HEADS, N_KEYS, PEER_KEY_HALF).astype(bf)
    xnt, e0, e1, theta = _routing(h1, norm2_g.reshape(1, d), peer_w_q.T.astype(bf), keys)
    vt = peer_v.astype(bf).reshape(-1, EXPERT_ROWS_I * N_KEYS, d).transpose(0, 2, 1)
    out = _experts(xnt, peer_u.astype(bf), vt, e0, e1, theta, h1, final_norm_g.reshape(1, d))
    return out.reshape(b, seq, d)
```

```python
import functools
import math

import jax
import jax.numpy as jnp
from jax import lax
from jax.experimental import pallas as pl
from jax.experimental.pallas import tpu as pltpu

N_META = 16
EPS = 1e-6
CONV_CH = 512
POOL_WINDOWS = (2, 4, 8, 16)
POOL_GROUP_DIM = 128
PEER_HEADS = 8
N_KEYS = 128
PEER_KEY_HALF = 128
PEER_TOPK = 16

LANES = 128
SUBLANES = 8
VMEM_LIMIT_BYTES = 56 * 1024 * 1024

HALO = 16
MIX_TILE = 1024
ROUTE_TILE = SUBLANES * LANES
GROUP_PITCH = N_KEYS + SUBLANES
EXPERT_TILE_TOKENS = 512
EXPERT_ROWS_I = 16
EXPERT_CHUNK_I = 4
GATE_BLOCK = 4

_SQRT_HALF = math.sqrt(0.5)


def _rms_norm_f32(v, g):
    return v * lax.rsqrt(jnp.mean(v * v, axis=-1, keepdims=True) + EPS) * g


def _mixer_kernel(seq_len, xprev_ref, x_ref, xnext_ref, meta_ref, g1_ref, win_ref, convw_ref,
                  convb_ref, poolw_ref, pscale_ref, wout_ref, h1_ref, hn_scr, p_scr, z_scr, y_scr):
    ts = x_ref.shape[1]
    j = pl.program_id(1)
    last = pl.num_programs(1) - 1
    g1 = g1_ref[...]

    x = x_ref[0]
    prev = jnp.where(j == 0, meta_ref[...], xprev_ref[0])
    nxt = jnp.where(j == last, 0.0, _rms_norm_f32(xnext_ref[0], g1))
    hn_scr[0:HALO, :] = _rms_norm_f32(prev, g1).astype(jnp.bfloat16)
    hn_scr[HALO:HALO + ts, :] = _rms_norm_f32(x, g1).astype(jnp.bfloat16)
    hn_scr[HALO + ts:HALO + ts + HALO, :] = nxt.astype(jnp.bfloat16)

    p_scr[...] = jnp.dot(hn_scr[...], win_ref[...], preferred_element_type=jnp.float32)

    z_scr[...] = p_scr[:, 2 * CONV_CH:3 * CONV_CH] * p_scr[:, 0:CONV_CH]
    conv = (z_scr[HALO - 1:HALO - 1 + ts, :] * convw_ref[0:1, :]
            + z_scr[HALO:HALO + ts, :] * convw_ref[1:2, :]
            + z_scr[HALO + 1:HALO + 1 + ts, :] * convw_ref[2:3, :]
            + convb_ref[...])
    ya = p_scr[HALO:HALO + ts, CONV_CH:2 * CONV_CH] * conv
    y_scr[:, 0:CONV_CH] = ya.astype(jnp.bfloat16)

    pos = j * ts + lax.broadcasted_iota(jnp.int32, (ts, 1), 0)
    hb0 = 3 * CONV_CH
    for g, w in enumerate(POOL_WINDOWS):
        c0 = hb0 + g * POOL_GROUP_DIM
        c1 = c0 + POOL_GROUP_DIM
        acc = p_scr[HALO - w // 2:HALO - w // 2 + ts, c0:c1]
        for k in range(-w // 2 + 1, w // 2):
            acc = acc + p_scr[HALO + k:HALO + k + ts, c0:c1]
        cnt = (w - jnp.maximum(pos + w // 2 - seq_len, 0)).astype(jnp.float32)
        mixed = acc / cnt - p_scr[HALO:HALO + ts, c0:c1]
        yb = jnp.dot(mixed.astype(jnp.bfloat16), poolw_ref[g], preferred_element_type=jnp.float32)
        yb = yb * pscale_ref[:, g * POOL_GROUP_DIM:(g + 1) * POOL_GROUP_DIM]
        y_scr[:, CONV_CH + g * POOL_GROUP_DIM:CONV_CH + (g + 1) * POOL_GROUP_DIM] = yb.astype(jnp.bfloat16)

    h1_ref[0] = x + jnp.dot(y_scr[...], wout_ref[...], preferred_element_type=jnp.float32)


def _mixer(x, meta, g1, win, convw, convb, poolw, pscale, wout):
    b, seq, d = x.shape
    ts = MIX_TILE
    assert seq % ts == 0 and ts % HALO == 0 and meta.shape[0] == HALO
    nt = seq // ts
    hb = ts // HALO
    nhb = seq // HALO
    rows = ts + 2 * HALO
    full = lambda shape: pl.BlockSpec(shape, lambda bi, j: (0,) * len(shape),
                                      pipeline_mode=pl.Buffered(1))
    return pl.pallas_call(
        functools.partial(_mixer_kernel, seq),
        grid=(b, nt),
        in_specs=[
            pl.BlockSpec((1, HALO, d), lambda bi, j: (bi, jnp.maximum(j * hb - 1, 0), 0)),
            pl.BlockSpec((1, ts, d), lambda bi, j: (bi, j, 0)),
            pl.BlockSpec((1, HALO, d), lambda bi, j: (bi, jnp.minimum((j + 1) * hb, nhb - 1), 0)),
            full(meta.shape), full(g1.shape), full(win.shape), full(convw.shape), full(convb.shape),
            full(poolw.shape), full(pscale.shape), full(wout.shape),
        ],
        out_specs=pl.BlockSpec((1, ts, d), lambda bi, j: (bi, j, 0)),
        out_shape=jax.ShapeDtypeStruct((b, seq, d), jnp.float32),
        scratch_shapes=[
            pltpu.VMEM((rows, d), jnp.bfloat16),
            pltpu.VMEM((rows, win.shape[1]), jnp.float32),
            pltpu.VMEM((rows, CONV_CH), jnp.float32),
            pltpu.VMEM((ts, d), jnp.bfloat16),
        ],
        compiler_params=pltpu.CompilerParams(
            dimension_semantics=("arbitrary", "arbitrary"), vmem_limit_bytes=VMEM_LIMIT_BYTES),
        name="mixer",
    )(x, x, x, meta, g1, win, convw, convb, poolw, pscale, wout)


def _cmpx(v, a, b):
    hi = jnp.maximum(v[a], v[b])
    lo = jnp.minimum(v[a], v[b])
    v[a], v[b] = hi, lo


def _oddeven_merge_sort_pairs(n):
    pairs = []
    p = 1
    while p < n:
        k = p
        while k >= 1:
            for jj in range(k % p, n - k, 2 * k):
                for i in range(min(k, n - jj - k)):
                    if (i + jj) // (2 * p) == (i + jj + k) // (2 * p):
                        pairs.append((i + jj, i + jj + k))
            k //= 2
        p *= 2
    return pairs


_SORT16 = _oddeven_merge_sort_pairs(PEER_TOPK)


def _sort_desc(v):
    v = list(v)
    for a, b in _SORT16:
        _cmpx(v, a, b)
    return v


def _bitonic_sort_desc(v):
    v = list(v)
    d = len(v) // 2
    while d >= 1:
        for i in range(len(v)):
            if i & d == 0:
                _cmpx(v, i, i + d)
        d //= 2
    return v


def _top16_select(xs, ys):
    n = len(xs)
    out = list(xs)
    for m, y in enumerate(ys):
        out[n - 1 - m] = jnp.maximum(out[n - 1 - m], y)
    return out


def _routing_kernel(h1_ref, g2_ref, wqt_ref, keys_ref, xnt_ref, e0_ref, e1_ref, theta_ref,
                    xnb_scr, slab_scr):
    tr = h1_ref.shape[0]
    ngroups = tr // LANES
    xn = _rms_norm_f32(h1_ref[...], g2_ref[...])
    xnb_scr[...] = xn
    xnt_ref[...] = xn.T

    def slab_rows(g):
        return pl.ds(g * GROUP_PITCH, N_KEYS)

    def load_key(hp, n):
        return slab_scr[hp, pl.ds(n, ngroups, stride=GROUP_PITCH), :]

    def top16(hp):
        best = None
        for blk in range(N_KEYS // PEER_TOPK):
            cur = _sort_desc([load_key(hp, blk * PEER_TOPK + m) for m in range(PEER_TOPK)])
            best = cur if best is None else _bitonic_sort_desc(_top16_select(best, cur))
        return best

    def head_body(h, carry):
        q0 = pl.multiple_of(h * 2 * PEER_KEY_HALF, 2 * PEER_KEY_HALF)
        qt = lax.dot_general(wqt_ref[pl.ds(q0, 2 * PEER_KEY_HALF), :], xnb_scr[...],
                             (((1,), (1,)), ((), ())),
                             preferred_element_type=jnp.float32).astype(jnp.bfloat16)
        for half in range(2):
            s = jnp.dot(keys_ref[2 * h + half], qt[half * PEER_KEY_HALF:(half + 1) * PEER_KEY_HALF, :],
                        preferred_element_type=jnp.float32)
            for g in range(ngroups):
                slab_scr[half, slab_rows(g), :] = s[:, g * LANES:(g + 1) * LANES]
        a = top16(0)
        b = top16(1)
        cmax = a[0] + b[0]
        rows = [[a[i] + b[jj] for jj in range(PEER_TOPK // (i + 1))] for i in range(PEER_TOPK)]
        cands = [(i, jj, rows[i][jj]) for i in range(PEER_TOPK) for jj in range(len(rows[i]))]
        best = rows[0]
        singles = [rows[i][0] for i in range(PEER_TOPK // 2, PEER_TOPK)]
        merge_rows = [rows[i] for i in range(1, PEER_TOPK // 2)] + [singles]
        for r, row in enumerate(merge_rows):
            best = _top16_select(best, row)
            if r + 1 < len(merge_rows):
                best = _bitonic_sort_desc(best)
        tau = functools.reduce(jnp.minimum, best)
        z = functools.reduce(lambda u, v: u + v, [jnp.exp(c - cmax) for c in best])
        inv_z = 0.5 / z
        e0s = [jnp.exp(a[i] - a[0]) * inv_z for i in range(PEER_TOPK)]
        e1s = [jnp.exp(b[jj] - b[0]) for jj in range(PEER_TOPK)]
        theta = None
        for i, jj, c in cands:
            pij = jnp.where(c >= tau, e0s[i] * e1s[jj], jnp.inf)
            theta = pij if theta is None else jnp.minimum(theta, pij)
        for g in range(ngroups):
            lanes = slice(g * LANES, (g + 1) * LANES)
            a0 = a[0][g:g + 1, :]
            b0 = b[0][g:g + 1, :]
            iz = inv_z[g:g + 1, :]
            e0_ref[h, :, lanes] = jnp.exp(slab_scr[0, slab_rows(g), :] - a0) * iz
            e1_ref[h, :, lanes] = jnp.exp(slab_scr[1, slab_rows(g), :] - b0)
            theta_ref[h, :, lanes] = theta[g:g + 1, :]
        return carry

    lax.fori_loop(0, PEER_HEADS, head_body, 0)


def _routing(h1, g2, wqt, keys):
    t, d = h1.shape
    tr = ROUTE_TILE
    assert t % tr == 0
    ngroups = tr // LANES
    full = lambda shape: pl.BlockSpec(shape, lambda i: (0,) * len(shape),
                                      pipeline_mode=pl.Buffered(1))
    return pl.pallas_call(
        _routing_kernel,
        grid=(t // tr,),
        in_specs=[pl.BlockSpec((tr, d), lambda i: (i, 0)), full(g2.shape), full(wqt.shape),
                  full(keys.shape)],
        out_specs=[
            pl.BlockSpec((d, tr), lambda i: (0, i)),
            pl.BlockSpec((PEER_HEADS, N_KEYS, tr), lambda i: (0, 0, i)),
            pl.BlockSpec((PEER_HEADS, N_KEYS, tr), lambda i: (0, 0, i)),
            pl.BlockSpec((PEER_HEADS, 1, tr), lambda i: (0, 0, i)),
        ],
        out_shape=[
            jax.ShapeDtypeStruct((d, t), jnp.float32),
            jax.ShapeDtypeStruct((PEER_HEADS, N_KEYS, t), jnp.float32),
            jax.ShapeDtypeStruct((PEER_HEADS, N_KEYS, t), jnp.float32),
            jax.ShapeDtypeStruct((PEER_HEADS, 1, t), jnp.float32),
        ],
        scratch_shapes=[
            pltpu.VMEM((tr, d), jnp.float32),
            pltpu.VMEM((2, ngroups * GROUP_PITCH, LANES), jnp.float32),
        ],
        compiler_params=pltpu.CompilerParams(
            dimension_semantics=("arbitrary",), vmem_limit_bytes=VMEM_LIMIT_BYTES),
        name="routing",
    )(h1, g2, wqt, keys)


def _expert_kernel(xnt_ref, u_ref, vt_ref, e0_ref, e1_ref, theta_ref, h1_ref, gf_ref, out_ref,
                   pre_scr, y_scr, acc_scr):
    e = pl.program_id(1)
    tt = xnt_ref.shape[1]

    @pl.when(e == 0)
    def _():
        acc_scr[...] = jnp.zeros_like(acc_scr)

    n_chunks = EXPERT_ROWS_I // EXPERT_CHUNK_I
    chunk_rows = EXPERT_CHUNK_I * N_KEYS

    def pre_activations(c):
        pre_scr[c % 2] = jnp.dot(u_ref[c * chunk_rows:(c + 1) * chunk_rows, :], xnt_ref[...],
                                 preferred_element_type=jnp.float32)

    def output_part(c):
        c0 = c * chunk_rows
        acc_scr[...] += jnp.dot(vt_ref[:, c0:c0 + chunk_rows], y_scr[c0:c0 + chunk_rows, :],
                                preferred_element_type=jnp.float32)

    pre_activations(0)
    for c in range(n_chunks + 1):
        for ic in range(EXPERT_CHUNK_I):
            if ic == 1 and c + 1 < n_chunks:
                pre_activations(c + 1)
            if ic == 2 and c >= 1:
                output_part(c - 1)
            if c == n_chunks:
                continue
            il = c * EXPERT_CHUNK_I + ic
            for tg in range(tt // LANES):
                lanes = slice(tg * LANES, (tg + 1) * LANES)
                for jb in range(N_KEYS // (GATE_BLOCK * SUBLANES)):
                    j0s = [(jb * GATE_BLOCK + k) * SUBLANES for k in range(GATE_BLOCK)]
                    gates = [None] * GATE_BLOCK
                    for h in range(PEER_HEADS):
                        e0v = jnp.broadcast_to(e0_ref[h, il:il + 1, lanes], (SUBLANES, LANES))
                        thv = jnp.broadcast_to(theta_ref[h, :, lanes], (SUBLANES, LANES))
                        for k, j0 in enumerate(j0s):
                            p = e0v * e1_ref[h, j0:j0 + SUBLANES, lanes]
                            sel = jnp.where(p >= thv, p, 0.0)
                            gates[k] = sel if gates[k] is None else gates[k] + sel
                    p0 = ic * N_KEYS + j0s[0]
                    a = pre_scr[c % 2, p0:p0 + GATE_BLOCK * SUBLANES, lanes].astype(jnp.bfloat16)
                    act = a * (1.0 + lax.erf(a * _SQRT_HALF))
                    gate = jnp.concatenate(gates, axis=0).astype(jnp.bfloat16)
                    r0 = il * N_KEYS + j0s[0]
                    y_scr[r0:r0 + GATE_BLOCK * SUBLANES, lanes] = gate * act

    @pl.when(e == pl.num_programs(1) - 1)
    def _():
        h2 = h1_ref[...] + acc_scr[...].T
        out_ref[...] = _rms_norm_f32(h2, gf_ref[...])


def _experts(xnt, u, vt, e0, e1, theta, h1, gf):
    d, t = xnt.shape
    tt = EXPERT_TILE_TOKENS
    rows = EXPERT_ROWS_I * N_KEYS
    n_exp = u.shape[0]
    assert t % tt == 0 and n_exp % rows == 0
    return pl.pallas_call(
        _expert_kernel,
        grid=(t // tt, n_exp // rows),
        in_specs=[
            pl.BlockSpec((d, tt), lambda i, e: (0, i)),
            pl.BlockSpec((rows, d), lambda i, e: (e, 0)),
            pl.BlockSpec((None, d, rows), lambda i, e: (e, 0, 0)),
            pl.BlockSpec((PEER_HEADS, EXPERT_ROWS_I, tt), lambda i, e: (0, e, i)),
            pl.BlockSpec((PEER_HEADS, N_KEYS, tt), lambda i, e: (0, 0, i)),
            pl.BlockSpec((PEER_HEADS, 1, tt), lambda i, e: (0, 0, i)),
            pl.BlockSpec((tt, d), lambda i, e: (i, 0)),
            pl.BlockSpec(gf.shape, lambda i, e: (0, 0)),
        ],
        out_specs=pl.BlockSpec((tt, d), lambda i, e: (i, 0)),
        out_shape=jax.ShapeDtypeStruct((t, d), jnp.float32),
        scratch_shapes=[
            pltpu.VMEM((2, EXPERT_CHUNK_I * N_KEYS, tt), jnp.float32),
            pltpu.VMEM((rows, tt), jnp.bfloat16),
            pltpu.VMEM((d, tt), jnp.float32),
        ],
        compiler_params=pltpu.CompilerParams(
            dimension_semantics=("arbitrary", "arbitrary"), vmem_limit_bytes=VMEM_LIMIT_BYTES),
        name="experts",
    )(xnt, u, vt, e0, e1, theta, h1, gf)


def kernel(x, meta_tokens, norm1_g, w_in, conv_w, conv_b, pool_w, pool_scale, w_out, norm2_g,
           peer_w_q, peer_keys, peer_u, peer_v, final_norm_g):
    b, seq, d = x.shape
    bf = jnp.bfloat16
    h1 = _mixer(x, meta_tokens, norm1_g.reshape(1, d), w_in.astype(bf), conv_w,
                conv_b.reshape(1, -1), pool_w.astype(bf), pool_scale.reshape(1, -1), w_out.astype(bf))
    h1 = h1.reshape(b * seq, d)
    keys = peer_keys.reshape(2 * PEER_HEADS, N_KEYS, PEER_KEY_HALF).astype(bf)
    xnt, e0, e1, theta = _routing(h1, norm2_g.reshape(1, d), peer_w_q.T, keys)
    vt = peer_v.astype(bf).reshape(-1, EXPERT_ROWS_I * N_KEYS, d).transpose(0, 2, 1)
    out = _experts(xnt, peer_u, vt, e0, e1, theta, h1, final_norm_g.reshape(1, d))
    return out.reshape(b, seq, d)
```

```python
import functools
import math

import jax
import jax.numpy as jnp
from jax import lax
from jax.experimental import pallas as pl
from jax.experimental.pallas import tpu as pltpu

N_META = 16
EPS = 1e-6
CONV_CH = 512
POOL_WINDOWS = (2, 4, 8, 16)
POOL_GROUP_DIM = 128
PEER_HEADS = 8
N_KEYS = 128
PEER_KEY_HALF = 128
PEER_TOPK = 16

LANES = 128
SUBLANES = 8
VMEM_LIMIT_BYTES = 56 * 1024 * 1024

HALO = 16
MIX_TILE = 1024
ROUTE_TILE = SUBLANES * LANES
GROUP_PITCH = N_KEYS + SUBLANES
EXPERT_TILE_TOKENS = 512
EXPERT_ROWS_I = 16
EXPERT_CHUNK_I = 4
GATE_BLOCK = 4

_SQRT_HALF = math.sqrt(0.5)


def _rms_norm_f32(v, g):
    return v * lax.rsqrt(jnp.mean(v * v, axis=-1, keepdims=True) + EPS) * g


def _mixer_kernel(seq_len, xprev_ref, x_ref, xnext_ref, meta_ref, g1_ref, win_ref, convw_ref,
                  convb_ref, poolw_ref, pscale_ref, wout_ref, h1_ref, hn_scr, p_scr, z_scr, y_scr):
    ts = x_ref.shape[1]
    j = pl.program_id(1)
    last = pl.num_programs(1) - 1
    g1 = g1_ref[...]

    x = x_ref[0]
    prev = jnp.where(j == 0, meta_ref[...], xprev_ref[0])
    nxt = jnp.where(j == last, 0.0, _rms_norm_f32(xnext_ref[0], g1))
    hn_scr[0:HALO, :] = _rms_norm_f32(prev, g1)
    hn_scr[HALO:HALO + ts, :] = _rms_norm_f32(x, g1)
    hn_scr[HALO + ts:HALO + ts + HALO, :] = nxt

    p_scr[...] = jnp.dot(hn_scr[...], win_ref[...], preferred_element_type=jnp.float32)

    z_scr[...] = p_scr[:, 2 * CONV_CH:3 * CONV_CH] * p_scr[:, 0:CONV_CH]
    conv = (z_scr[HALO - 1:HALO - 1 + ts, :] * convw_ref[0:1, :]
            + z_scr[HALO:HALO + ts, :] * convw_ref[1:2, :]
            + z_scr[HALO + 1:HALO + 1 + ts, :] * convw_ref[2:3, :]
            + convb_ref[...])
    ya = p_scr[HALO:HALO + ts, CONV_CH:2 * CONV_CH] * conv
    y_scr[:, 0:CONV_CH] = ya.astype(jnp.bfloat16)

    pos = j * ts + lax.broadcasted_iota(jnp.int32, (ts, 1), 0)
    hb0 = 3 * CONV_CH
    for g, w in enumerate(POOL_WINDOWS):
        c0 = hb0 + g * POOL_GROUP_DIM
        c1 = c0 + POOL_GROUP_DIM
        acc = p_scr[HALO - w // 2:HALO - w // 2 + ts, c0:c1]
        for k in range(-w // 2 + 1, w // 2):
            acc = acc + p_scr[HALO + k:HALO + k + ts, c0:c1]
        cnt = (w - jnp.maximum(pos + w // 2 - seq_len, 0)).astype(jnp.float32)
        mixed = acc / cnt - p_scr[HALO:HALO + ts, c0:c1]
        yb = jnp.dot(mixed.astype(jnp.bfloat16), poolw_ref[g], preferred_element_type=jnp.float32)
        yb = yb * pscale_ref[:, g * POOL_GROUP_DIM:(g + 1) * POOL_GROUP_DIM]
        y_scr[:, CONV_CH + g * POOL_GROUP_DIM:CONV_CH + (g + 1) * POOL_GROUP_DIM] = yb.astype(jnp.bfloat16)

    h1_ref[0] = x + jnp.dot(y_scr[...], wout_ref[...], preferred_element_type=jnp.float32)


def _mixer(x, meta, g1, win, convw, convb, poolw, pscale, wout):
    b, seq, d = x.shape
    ts = MIX_TILE
    assert seq % ts == 0 and ts % HALO == 0 and meta.shape[0] == HALO
    nt = seq // ts
    hb = ts // HALO
    nhb = seq // HALO
    rows = ts + 2 * HALO
    full = lambda shape: pl.BlockSpec(shape, lambda bi, j: (0,) * len(shape),
                                      pipeline_mode=pl.Buffered(1))
    return pl.pallas_call(
        functools.partial(_mixer_kernel, seq),
        grid=(b, nt),
        in_specs=[
            pl.BlockSpec((1, HALO, d), lambda bi, j: (bi, jnp.maximum(j * hb - 1, 0), 0)),
            pl.BlockSpec((1, ts, d), lambda bi, j: (bi, j, 0)),
            pl.BlockSpec((1, HALO, d), lambda bi, j: (bi, jnp.minimum((j + 1) * hb, nhb - 1), 0)),
            full(meta.shape), full(g1.shape), full(win.shape), full(convw.shape), full(convb.shape),
            full(poolw.shape), full(pscale.shape), full(wout.shape),
        ],
        out_specs=pl.BlockSpec((1, ts, d), lambda bi, j: (bi, j, 0)),
        out_shape=jax.ShapeDtypeStruct((b, seq, d), jnp.float32),
        scratch_shapes=[
            pltpu.VMEM((rows, d), jnp.float32),
            pltpu.VMEM((rows, win.shape[1]), jnp.float32),
            pltpu.VMEM((rows, CONV_CH), jnp.float32),
            pltpu.VMEM((ts, d), jnp.bfloat16),
        ],
        compiler_params=pltpu.CompilerParams(
            dimension_semantics=("arbitrary", "arbitrary"), vmem_limit_bytes=VMEM_LIMIT_BYTES),
        name="mixer",
    )(x, x, x, meta, g1, win, convw, convb, poolw, pscale, wout)


def _cmpx(v, a, b):
    hi = jnp.maximum(v[a], v[b])
    lo = jnp.minimum(v[a], v[b])
    v[a], v[b] = hi, lo


def _oddeven_merge_sort_pairs(n):
    pairs = []
    p = 1
    while p < n:
        k = p
        while k >= 1:
            for jj in range(k % p, n - k, 2 * k):
                for i in range(min(k, n - jj - k)):
                    if (i + jj) // (2 * p) == (i + jj + k) // (2 * p):
                        pairs.append((i + jj, i + jj + k))
            k //= 2
        p *= 2
    return pairs


_SORT16 = _oddeven_merge_sort_pairs(PEER_TOPK)


def _sort_desc(v):
    v = list(v)
    for a, b in _SORT16:
        _cmpx(v, a, b)
    return v


def _bitonic_sort_desc(v):
    v = list(v)
    d = len(v) // 2
    while d >= 1:
        for i in range(len(v)):
            if i & d == 0:
                _cmpx(v, i, i + d)
        d //= 2
    return v


def _top16_select(xs, ys):
    n = len(xs)
    out = list(xs)
    for m, y in enumerate(ys):
        out[n - 1 - m] = jnp.maximum(out[n - 1 - m], y)
    return out


def _routing_kernel(h1_ref, g2_ref, wqt_ref, keys_ref, xnt_ref, e0_ref, e1_ref, theta_ref,
                    xnb_scr, slab_scr):
    tr = h1_ref.shape[0]
    ngroups = tr // LANES
    xn = _rms_norm_f32(h1_ref[...], g2_ref[...])
    xnb_scr[...] = xn.astype(jnp.bfloat16)
    xnt_ref[...] = xn.T

    def slab_rows(g):
        return pl.ds(g * GROUP_PITCH, N_KEYS)

    def load_key(hp, n):
        return slab_scr[hp, pl.ds(n, ngroups, stride=GROUP_PITCH), :]

    def top16(hp):
        best = None
        for blk in range(N_KEYS // PEER_TOPK):
            cur = _sort_desc([load_key(hp, blk * PEER_TOPK + m) for m in range(PEER_TOPK)])
            best = cur if best is None else _bitonic_sort_desc(_top16_select(best, cur))
        return best

    def head_body(h, carry):
        q0 = pl.multiple_of(h * 2 * PEER_KEY_HALF, 2 * PEER_KEY_HALF)
        qt = lax.dot_general(wqt_ref[pl.ds(q0, 2 * PEER_KEY_HALF), :], xnb_scr[...],
                             (((1,), (1,)), ((), ())),
                             preferred_element_type=jnp.float32).astype(jnp.bfloat16)
        for half in range(2):
            s = jnp.dot(keys_ref[2 * h + half], qt[half * PEER_KEY_HALF:(half + 1) * PEER_KEY_HALF, :],
                        preferred_element_type=jnp.float32)
            for g in range(ngroups):
                slab_scr[half, slab_rows(g), :] = s[:, g * LANES:(g + 1) * LANES]
        a = top16(0)
        b = top16(1)
        cmax = a[0] + b[0]
        rows = [[a[i] + b[jj] for jj in range(PEER_TOPK // (i + 1))] for i in range(PEER_TOPK)]
        cands = [(i, jj, rows[i][jj]) for i in range(PEER_TOPK) for jj in range(len(rows[i]))]
        best = rows[0]
        singles = [rows[i][0] for i in range(PEER_TOPK // 2, PEER_TOPK)]
        merge_rows = [rows[i] for i in range(1, PEER_TOPK // 2)] + [singles]
        for r, row in enumerate(merge_rows):
            best = _top16_select(best, row)
            if r + 1 < len(merge_rows):
                best = _bitonic_sort_desc(best)
        tau = functools.reduce(jnp.minimum, best)
        z = functools.reduce(lambda u, v: u + v, [jnp.exp(c - cmax) for c in best])
        inv_z = 0.5 / z
        e0s = [jnp.exp(a[i] - a[0]) * inv_z for i in range(PEER_TOPK)]
        e1s = [jnp.exp(b[jj] - b[0]) for jj in range(PEER_TOPK)]
        theta = None
        for i, jj, c in cands:
            pij = jnp.where(c >= tau, e0s[i] * e1s[jj], jnp.inf)
            theta = pij if theta is None else jnp.minimum(theta, pij)
        for g in range(ngroups):
            lanes = slice(g * LANES, (g + 1) * LANES)
            a0 = a[0][g:g + 1, :]
            b0 = b[0][g:g + 1, :]
            iz = inv_z[g:g + 1, :]
            e0_ref[h, :, lanes] = jnp.exp(slab_scr[0, slab_rows(g), :] - a0) * iz
            e1_ref[h, :, lanes] = jnp.exp(slab_scr[1, slab_rows(g), :] - b0)
            theta_ref[h, :, lanes] = theta[g:g + 1, :]
        return carry

    lax.fori_loop(0, PEER_HEADS, head_body, 0)


def _routing(h1, g2, wqt, keys):
    t, d = h1.shape
    tr = ROUTE_TILE
    assert t % tr == 0
    ngroups = tr // LANES
    full = lambda shape: pl.BlockSpec(shape, lambda i: (0,) * len(shape),
                                      pipeline_mode=pl.Buffered(1))
    return pl.pallas_call(
        _routing_kernel,
        grid=(t // tr,),
        in_specs=[pl.BlockSpec((tr, d), lambda i: (i, 0)), full(g2.shape), full(wqt.shape),
                  full(keys.shape)],
        out_specs=[
            pl.BlockSpec((d, tr), lambda i: (0, i)),
            pl.BlockSpec((PEER_HEADS, N_KEYS, tr), lambda i: (0, 0, i)),
            pl.BlockSpec((PEER_HEADS, N_KEYS, tr), lambda i: (0, 0, i)),
            pl.BlockSpec((PEER_HEADS, 1, tr), lambda i: (0, 0, i)),
        ],
        out_shape=[
            jax.ShapeDtypeStruct((d, t), jnp.float32),
            jax.ShapeDtypeStruct((PEER_HEADS, N_KEYS, t), jnp.float32),
            jax.ShapeDtypeStruct((PEER_HEADS, N_KEYS, t), jnp.float32),
            jax.ShapeDtypeStruct((PEER_HEADS, 1, t), jnp.float32),
        ],
        scratch_shapes=[
            pltpu.VMEM((tr, d), jnp.bfloat16),
            pltpu.VMEM((2, ngroups * GROUP_PITCH, LANES), jnp.float32),
        ],
        compiler_params=pltpu.CompilerParams(
            dimension_semantics=("arbitrary",), vmem_limit_bytes=VMEM_LIMIT_BYTES),
        name="routing",
    )(h1, g2, wqt, keys)


def _expert_kernel(xnt_ref, u_ref, vt_ref, e0_ref, e1_ref, theta_ref, h1_ref, gf_ref, out_ref,
                   pre_scr, y_scr, acc_scr):
    e = pl.program_id(1)
    tt = xnt_ref.shape[1]

    @pl.when(e == 0)
    def _():
        acc_scr[...] = jnp.zeros_like(acc_scr)

    n_chunks = EXPERT_ROWS_I // EXPERT_CHUNK_I
    chunk_rows = EXPERT_CHUNK_I * N_KEYS

    def pre_activations(c):
        pre_scr[c % 2] = jnp.dot(u_ref[c * chunk_rows:(c + 1) * chunk_rows, :], xnt_ref[...],
                                 preferred_element_type=jnp.float32)

    def output_part(c):
        c0 = c * chunk_rows
        acc_scr[...] += jnp.dot(vt_ref[:, c0:c0 + chunk_rows], y_scr[c0:c0 + chunk_rows, :],
                                preferred_element_type=jnp.float32)

    pre_activations(0)
    for c in range(n_chunks + 1):
        for ic in range(EXPERT_CHUNK_I):
            if ic == 1 and c + 1 < n_chunks:
                pre_activations(c + 1)
            if ic == 2 and c >= 1:
                output_part(c - 1)
            if c == n_chunks:
                continue
            il = c * EXPERT_CHUNK_I + ic
            for tg in range(tt // LANES):
                lanes = slice(tg * LANES, (tg + 1) * LANES)
                for jb in range(N_KEYS // (GATE_BLOCK * SUBLANES)):
                    j0s = [(jb * GATE_BLOCK + k) * SUBLANES for k in range(GATE_BLOCK)]
                    gates = [None] * GATE_BLOCK
                    for h in range(PEER_HEADS):
                        e0v = jnp.broadcast_to(e0_ref[h, il:il + 1, lanes], (SUBLANES, LANES))
                        thv = jnp.broadcast_to(theta_ref[h, :, lanes], (SUBLANES, LANES))
                        for k, j0 in enumerate(j0s):
                            p = e0v * e1_ref[h, j0:j0 + SUBLANES, lanes]
                            sel = jnp.where(p >= thv, p, 0.0)
                            gates[k] = sel if gates[k] is None else gates[k] + sel
                    p0 = ic * N_KEYS + j0s[0]
                    a = pre_scr[c % 2, p0:p0 + GATE_BLOCK * SUBLANES, lanes].astype(jnp.bfloat16)
                    act = a * (1.0 + lax.erf(a * _SQRT_HALF))
                    gate = jnp.concatenate(gates, axis=0).astype(jnp.bfloat16)
                    r0 = il * N_KEYS + j0s[0]
                    y_scr[r0:r0 + GATE_BLOCK * SUBLANES, lanes] = gate * act

    @pl.when(e == pl.num_programs(1) - 1)
    def _():
        h2 = h1_ref[...] + acc_scr[...].T
        out_ref[...] = _rms_norm_f32(h2, gf_ref[...])


def _experts(xnt, u, vt, e0, e1, theta, h1, gf):
    d, t = xnt.shape
    tt = EXPERT_TILE_TOKENS
    rows = EXPERT_ROWS_I * N_KEYS
    n_exp = u.shape[0]
    assert t % tt == 0 and n_exp % rows == 0
    return pl.pallas_call(
        _expert_kernel,
        grid=(t // tt, n_exp // rows),
        in_specs=[
            pl.BlockSpec((d, tt), lambda i, e: (0, i)),
            pl.BlockSpec((rows, d), lambda i, e: (e, 0)),
            pl.BlockSpec((None, d, rows), lambda i, e: (e, 0, 0)),
            pl.BlockSpec((PEER_HEADS, EXPERT_ROWS_I, tt), lambda i, e: (0, e, i)),
            pl.BlockSpec((PEER_HEADS, N_KEYS, tt), lambda i, e: (0, 0, i)),
            pl.BlockSpec((PEER_HEADS, 1, tt), lambda i, e: (0, 0, i)),
            pl.BlockSpec((tt, d), lambda i, e: (i, 0)),
            pl.BlockSpec(gf.shape, lambda i, e: (0, 0)),
        ],
        out_specs=pl.BlockSpec((tt, d), lambda i, e: (i, 0)),
        out_shape=jax.ShapeDtypeStruct((t, d), jnp.float32),
        scratch_shapes=[
            pltpu.VMEM((2, EXPERT_CHUNK_I * N_KEYS, tt), jnp.float32),
            pltpu.VMEM((rows, tt), jnp.bfloat16),
            pltpu.VMEM((d, tt), jnp.float32),
        ],
        compiler_params=pltpu.CompilerParams(
            dimension_semantics=("arbitrary", "arbitrary"), vmem_limit_bytes=VMEM_LIMIT_BYTES),
        name="experts",
    )(xnt, u, vt, e0, e1, theta, h1, gf)


def kernel(x, meta_tokens, norm1_g, w_in, conv_w, conv_b, pool_w, pool_scale, w_out, norm2_g,
           peer_w_q, peer_keys, peer_u, peer_v, final_norm_g):
    b, seq, d = x.shape
    bf = jnp.bfloat16
    h1 = _mixer(x, meta_tokens, norm1_g.reshape(1, d), w_in, conv_w,
                conv_b.reshape(1, -1), pool_w.astype(bf), pool_scale.reshape(1, -1), w_out.astype(bf))
    h1 = h1.reshape(b * seq, d)
    keys = peer_keys.reshape(2 * PEER_HEADS, N_KEYS, PEER_KEY_HALF).astype(bf)
    xnt, e0, e1, theta = _routing(h1, norm2_g.reshape(1, d), peer_w_q.T.astype(bf), keys)
    vt = peer_v.astype(bf).reshape(-1, EXPERT_ROWS_I * N_KEYS, d).transpose(0, 2, 1)
    out = _experts(xnt, peer_u, vt, e0, e1, theta, h1, final_norm_g.reshape(1, d))
    return out.reshape(b, seq, d)
```
